```python
import math
import jax, jax.numpy as jnp
from jax import lax
import numpy as np

D_MODEL = 1024
BATCH = 4
SEQ = 8192
DEPTH = 1

D_MIX = D_MODEL
CONV_CH = D_MIX // 2
CONV_GROUPS = 8
CONV_WIDTH = 31
SB_HEADS = 8
SB_HEAD_DIM = 64
SB_WIDTH = SB_HEADS * SB_HEAD_DIM
Q_BLOCK = 128
D_FF = int(math.ceil((8 * D_MODEL / 3) / 256) * 256)
IN_COLS = 2 * CONV_CH + 3 * SB_WIDTH
EPS = 1e-6

kernel_name = "hymba_conformer_stickbreaking_sandwich"


def _rmsnorm(x, g):
    xf = x.astype(jnp.float32)
    y = xf * lax.rsqrt(jnp.mean(xf * xf, axis=-1, keepdims=True) + EPS)
    return (y * g.astype(jnp.float32)).astype(x.dtype)


def _layernorm(x, g, b):
    xf = x.astype(jnp.float32)
    mu = jnp.mean(xf, axis=-1, keepdims=True)
    var = jnp.mean(jnp.square(xf - mu), axis=-1, keepdims=True)
    y = (xf - mu) * lax.rsqrt(var + EPS)
    return (y * g.astype(jnp.float32) + b.astype(jnp.float32)).astype(x.dtype)


def _conformer_conv(val, gate, conv_w, conv_b, ln_g, ln_b):
    glu = val * jax.nn.sigmoid(gate)
    padded = jnp.pad(glu, ((0, 0), (CONV_WIDTH - 1, 0), (0, 0)))
    y = lax.conv_general_dilated(
        padded, conv_w.astype(glu.dtype), window_strides=(1,), padding="VALID",
        dimension_numbers=("NWC", "WIO", "NWC"), feature_group_count=CONV_CH)
    y = y + conv_b
    y = _layernorm(y, ln_g, ln_b)
    return jax.nn.silu(y)


def _stick_breaking(q, k, v):
    B, S = q.shape[0], q.shape[1]
    scale = 1.0 / math.sqrt(SB_HEAD_DIM)
    qh = jnp.transpose(q, (0, 2, 1, 3))
    kh = jnp.transpose(k, (0, 2, 1, 3))
    vh = jnp.transpose(v, (0, 2, 1, 3))
    nb = S // Q_BLOCK
    q_blocks = jnp.transpose(qh.reshape(B, SB_HEADS, nb, Q_BLOCK, SB_HEAD_DIM), (2, 0, 1, 3, 4))
    starts = jnp.arange(nb, dtype=jnp.int32) * Q_BLOCK
    key_pos = jnp.arange(S, dtype=jnp.int32)

    def one_block(args):
        qb, t0 = args
        z = jnp.einsum("bhqd,bhkd->bhqk", qb, kh,
                       preferred_element_type=jnp.float32) * scale
        q_pos = t0 + jnp.arange(Q_BLOCK, dtype=jnp.int32)
        mask = key_pos[None, :] < q_pos[:, None]
        log_beta = jax.nn.log_sigmoid(z)
        log_1m_beta = jnp.where(mask, jax.nn.log_sigmoid(-z), 0.0)
        between = lax.cumsum(log_1m_beta, axis=3, reverse=True) - log_1m_beta
        attn = jnp.where(mask, jnp.exp(log_beta + between), 0.0)
        return jnp.einsum("bhqk,bhkd->bhqd", attn.astype(vh.dtype), vh)

    out = lax.map(one_block, (q_blocks, starts))
    out = jnp.transpose(out, (1, 0, 3, 2, 4))
    return out.reshape(B, S, SB_HEADS, SB_HEAD_DIM)


def setup_inputs(seed: int = 0) -> dict:
    key = jax.random.key(seed)
    ks = jax.random.split(key, 16)
    nrm = jax.random.normal
    def gain(k, shape):
        return 1.0 + 0.05 * nrm(k, shape, jnp.float32)
    return {
        "x": nrm(ks[0], (BATCH, SEQ, D_MODEL), jnp.float32),
        "g_pre_mix": gain(ks[1], (DEPTH, D_MODEL)),
        "w_in": nrm(ks[2], (DEPTH, D_MODEL, IN_COLS), jnp.float32) * D_MODEL ** -0.5,
        "conv_w": nrm(ks[3], (DEPTH, CONV_WIDTH, 1, CONV_CH), jnp.float32) * CONV_WIDTH ** -0.5,
        "conv_b": 0.02 * nrm(ks[4], (DEPTH, CONV_CH), jnp.float32),
        "conv_ln_g": gain(ks[5], (DEPTH, CONV_CH)),
        "conv_ln_b": 0.02 * nrm(ks[6], (DEPTH, CONV_CH), jnp.float32),
        "attn_norm_g": gain(ks[7], (DEPTH, SB_HEADS, SB_HEAD_DIM)),
        "w_out": nrm(ks[8], (DEPTH, D_MIX, D_MODEL), jnp.float32) * D_MIX ** -0.5,
        "g_post_mix": gain(ks[9], (DEPTH, D_MODEL)),
        "g_pre_ffn": gain(ks[10], (DEPTH, D_MODEL)),
        "w_gate": nrm(ks[11], (DEPTH, D_MODEL, D_FF), jnp.float32) * D_MODEL ** -0.5,
        "w_up": nrm(ks[12], (DEPTH, D_MODEL, D_FF), jnp.float32) * D_MODEL ** -0.5,
        "w_down": nrm(ks[13], (DEPTH, D_FF, D_MODEL), jnp.float32) * D_FF ** -0.5,
        "g_post_ffn": gain(ks[14], (DEPTH, D_MODEL)),
    }


def reference(x, g_pre_mix, w_in, conv_w, conv_b, conv_ln_g, conv_ln_b, attn_norm_g,
              w_out, g_post_mix, g_pre_ffn, w_gate, w_up, w_down, g_post_ffn):
    B, S, _ = x.shape
    h = x
    for l in range(DEPTH):
        a = _rmsnorm(h, g_pre_mix[l])
        u = jnp.einsum("bsd,dc->bsc", a, w_in[l])
        c_val = u[..., :CONV_CH]
        c_gate = u[..., CONV_CH:2 * CONV_CH]
        qkv = u[..., 2 * CONV_CH:].reshape(B, S, 3, SB_HEADS, SB_HEAD_DIM)
        conv_out = _conformer_conv(c_val, c_gate, conv_w[l], conv_b[l], conv_ln_g[l], conv_ln_b[l])
        attn_out = _stick_breaking(qkv[:, :, 0], qkv[:, :, 1], qkv[:, :, 2])
        attn_out = _rmsnorm(attn_out, attn_norm_g[l]).reshape(B, S, SB_WIDTH)
        mixed = jnp.concatenate([conv_out, attn_out], axis=-1)
        y = jnp.einsum("bsc,cd->bsd", mixed, w_out[l])
        h = h + _rmsnorm(y, g_post_mix[l])
        f_in = _rmsnorm(h, g_pre_ffn[l])
        gt = jnp.einsum("bsd,df->bsf", f_in, w_gate[l])
        up = jnp.einsum("bsd,df->bsf", f_in, w_up[l])
        f = jnp.einsum("bsf,fd->bsd", jax.nn.silu(gt) * up, w_down[l])
        h = h + _rmsnorm(f, g_post_ffn[l])
    return h
```

```python
import functools
import math

import jax
import jax.numpy as jnp
from jax import lax
from jax.experimental import pallas as pl
from jax.experimental.pallas import tpu as pltpu

EPS = 1e-6
CONV_CH = 512
CONV_WIDTH = 31
SB_HEADS = 8
SB_HEAD_DIM = 64
SB_WIDTH = SB_HEADS * SB_HEAD_DIM
HEAD_PAIR = 2 * SB_HEAD_DIM
N_PAIRS = SB_HEADS // 2
CONV_HALO = 32

VMEM_LIMIT_BYTES = 56 * 1024 * 1024

F32 = jnp.float32
BF16 = jnp.bfloat16


def _rms(x, g):
    return x * lax.rsqrt(jnp.mean(x * x, axis=-1, keepdims=True) + EPS) * g


def _in_proj_kernel(x_ref, g_ref, w_ref, glu_ref, q_ref, k_ref, v_ref, *, q_scale):
    a = _rms(x_ref[...], g_ref[...])
    u = jnp.dot(a.astype(BF16), w_ref[...], preferred_element_type=F32)
    val = u[:, :CONV_CH]
    gate = u[:, CONV_CH:2 * CONV_CH]
    glu_ref[...] = val * jax.nn.sigmoid(gate)
    o = 2 * CONV_CH
    q_ref[...] = (u[:, o:o + SB_WIDTH] * q_scale).astype(BF16)
    k_ref[...] = u[:, o + SB_WIDTH:o + 2 * SB_WIDTH].astype(BF16)
    v_ref[...] = u[:, o + 2 * SB_WIDTH:o + 3 * SB_WIDTH].astype(BF16)


def _in_proj(x2, g, w_bf16, tm):
    n, d = x2.shape
    cols = w_bf16.shape[1]
    kern = functools.partial(_in_proj_kernel, q_scale=1.0 / math.sqrt(SB_HEAD_DIM))
    return pl.pallas_call(
        kern,
        grid=(n // tm,),
        in_specs=[
            pl.BlockSpec((tm, d), lambda i: (i, 0)),
            pl.BlockSpec((1, d), lambda i: (0, 0)),
            pl.BlockSpec((d, cols), lambda i: (0, 0)),
        ],
        out_specs=[
            pl.BlockSpec((tm, CONV_CH), lambda i: (i, 0)),
            pl.BlockSpec((tm, SB_WIDTH), lambda i: (i, 0)),
            pl.BlockSpec((tm, SB_WIDTH), lambda i: (i, 0)),
            pl.BlockSpec((tm, SB_WIDTH), lambda i: (i, 0)),
        ],
        out_shape=[
            jax.ShapeDtypeStruct((n, CONV_CH), F32),
            jax.ShapeDtypeStruct((n, SB_WIDTH), BF16),
            jax.ShapeDtypeStruct((n, SB_WIDTH), BF16),
            jax.ShapeDtypeStruct((n, SB_WIDTH), BF16),
        ],
        compiler_params=pltpu.CompilerParams(
            dimension_semantics=("arbitrary",), vmem_limit_bytes=VMEM_LIMIT_BYTES),
        name="in_proj",
    )(x2, g, w_bf16)


def _conv_kernel(glu_ref, halo_ref, cw_ref, cb_ref, lg_ref, lb_ref, o_ref, buf_ref, *, ts, rc):
    i = pl.program_id(1)
    buf_ref[0:CONV_HALO, :] = jnp.where(i > 0, halo_ref[0], 0.0)
    buf_ref[CONV_HALO:, :] = glu_ref[0]
    cb = cb_ref[...]
    lg = lg_ref[...]
    lb = lb_ref[...]
    base = CONV_HALO - (CONV_WIDTH - 1)
    for r in range(0, ts, rc):
        acc = jnp.broadcast_to(cb, (rc, CONV_CH))
        for w in range(CONV_WIDTH):
            acc = acc + buf_ref[pl.ds(base + r + w, rc), :] * cw_ref[w:w + 1, :]
        mu = jnp.mean(acc, axis=-1, keepdims=True)
        cen = acc - mu
        var = jnp.mean(cen * cen, axis=-1, keepdims=True)
        y = cen * lax.rsqrt(var + EPS) * lg + lb
        o_ref[0, r:r + rc, :] = (y * jax.nn.sigmoid(y)).astype(o_ref.dtype)


def _conv(glu, cw, cb, lg, lb, ts, rc):
    b, s, c = glu.shape
    kern = functools.partial(_conv_kernel, ts=ts, rc=rc)
    hb = ts // CONV_HALO
    return pl.pallas_call(
        kern,
        grid=(b, s // ts),
        in_specs=[
            pl.BlockSpec((1, ts, c), lambda bi, i: (bi, i, 0)),
            pl.BlockSpec((1, CONV_HALO, c), lambda bi, i: (bi, jnp.maximum(i * hb - 1, 0), 0)),
            pl.BlockSpec((CONV_WIDTH, c), lambda bi, i: (0, 0)),
            pl.BlockSpec((1, c), lambda bi, i: (0, 0)),
            pl.BlockSpec((1, c), lambda bi, i: (0, 0)),
            pl.BlockSpec((1, c), lambda bi, i: (0, 0)),
        ],
        out_specs=pl.BlockSpec((1, ts, c), lambda bi, i: (bi, i, 0)),
        out_shape=jax.ShapeDtypeStruct((b, s, c), BF16),
        scratch_shapes=[pltpu.VMEM((ts + CONV_HALO, c), F32)],
        compiler_params=pltpu.CompilerParams(
            dimension_semantics=("arbitrary", "arbitrary"), vmem_limit_bytes=VMEM_LIMIT_BYTES),
        name="conv",
    )(glu, glu, cw, cb, lg, lb)


def _attn_kernel(q_ref, k_ref, v_ref, tri_ref, g_ref, o_ref, *, t):
    i = pl.program_id(2)
    q2 = q_ref[0]
    lane = lax.broadcasted_iota(jnp.int32, (1, HEAD_PAIR), 1)
    first = lane < SB_HEAD_DIM
    zero = jnp.zeros((), BF16)
    q_heads = (jnp.where(first, q2, zero), jnp.where(first, zero, q2))
    tri = tri_ref[...]
    row = lax.broadcasted_iota(jnp.int32, (t, t), 0)
    col = lax.broadcasted_iota(jnp.int32, (t, t), 1)
    causal = col < row

    def sweep(j, acc, carries, masked):
        start = pl.multiple_of(j * t, t)
        kt = k_ref[0, pl.ds(start, t), :]
        vt = v_ref[0, pl.ds(start, t), :]
        v_heads = (jnp.where(first, vt, zero), jnp.where(first, zero, vt))
        new_carries = []
        for h in range(2):
            z = lax.dot_general(q_heads[h], kt, (((1,), (1,)), ((), ())),
                                preferred_element_type=F32)
            sp = jnp.maximum(z, 0.0) + jnp.log1p(jnp.exp(-jnp.abs(z)))
            lb = z - sp
            if masked:
                sp = jnp.where(causal, sp, 0.0)
            hi = sp.astype(BF16)
            lo = (sp - hi.astype(F32)).astype(BF16)
            cs = (jnp.dot(hi, tri, preferred_element_type=F32)
                  + jnp.dot(lo, tri, preferred_element_type=F32))
            p = jnp.exp(lb - cs - carries[h])
            if masked:
                p = jnp.where(causal, p, 0.0)
            acc = acc + jnp.dot(p.astype(BF16), v_heads[h], preferred_element_type=F32)
            new_carries.append(carries[h] + jnp.sum(sp, axis=1, keepdims=True))
        return acc, tuple(new_carries)

    acc0 = jnp.zeros((t, HEAD_PAIR), F32)
    c0 = jnp.zeros((t, 1), F32)
    acc, carries = sweep(i, acc0, (c0, c0), True)

    def body(jj, state):
        acc, ca, cb = state
        acc, (ca, cb) = sweep(i - 1 - jj, acc, (ca, cb), False)
        return acc, ca, cb

    acc, _, _ = lax.fori_loop(0, i, body, (acc, carries[0], carries[1]))

    sq = acc * acc
    ms0 = jnp.sum(jnp.where(first, sq, 0.0), axis=-1, keepdims=True) * (1.0 / SB_HEAD_DIM)
    ms1 = jnp.sum(jnp.where(first, 0.0, sq), axis=-1, keepdims=True) * (1.0 / SB_HEAD_DIM)
    inv = jnp.where(first, lax.rsqrt(ms0 + EPS), lax.rsqrt(ms1 + EPS))
    o_ref[0] = (acc * inv * g_ref[0]).astype(o_ref.dtype)


def _attention(q, k, v, g_pairs, t):
    b, s, _ = q.shape
    tri = (lax.broadcasted_iota(jnp.int32, (t, t), 0)
           > lax.broadcasted_iota(jnp.int32, (t, t), 1)).astype(BF16)
    kern = functools.partial(_attn_kernel, t=t)
    return pl.pallas_call(
        kern,
        grid=(b, N_PAIRS, s // t),
        in_specs=[
            pl.BlockSpec((1, t, HEAD_PAIR), lambda bi, p, i: (bi, i, p)),
            pl.BlockSpec((1, s, HEAD_PAIR), lambda bi, p, i: (bi, 0, p)),
            pl.BlockSpec((1, s, HEAD_PAIR), lambda bi, p, i: (bi, 0, p)),
            pl.BlockSpec((t, t), lambda bi, p, i: (0, 0)),
            pl.BlockSpec((1, 1, HEAD_PAIR), lambda bi, p, i: (p, 0, 0)),
        ],
        out_specs=pl.BlockSpec((1, t, HEAD_PAIR), lambda bi, p, i: (bi, i, p)),
        out_shape=jax.ShapeDtypeStruct((b, s, SB_WIDTH), BF16),
        compiler_params=pltpu.CompilerParams(
            dimension_semantics=("arbitrary", "arbitrary", "arbitrary"),
            vmem_limit_bytes=VMEM_LIMIT_BYTES),
        name="attn",
    )(q, k, v, tri, g_pairs)


def _out_ffn_kernel(x_ref, conv_ref, attn_ref, wo_ref, gpm_ref, gpf_ref, wg_ref, wu_ref, wd_ref,
                    gpo_ref, o_ref):
    y = (jnp.dot(conv_ref[...], wo_ref[0:CONV_CH, :], preferred_element_type=F32)
         + jnp.dot(attn_ref[...], wo_ref[CONV_CH:, :], preferred_element_type=F32))
    h = x_ref[...] + _rms(y, gpm_ref[...])
    f_in = _rms(h, gpf_ref[...]).astype(BF16)
    gt = jnp.dot(f_in, wg_ref[...], preferred_element_type=F32)
    up = jnp.dot(f_in, wu_ref[...], preferred_element_type=F32)
    act = (gt * jax.nn.sigmoid(gt) * up).astype(BF16)
    f = jnp.dot(act, wd_ref[...], preferred_element_type=F32)
    o_ref[...] = h + _rms(f, gpo_ref[...])


def _out_ffn(x2, conv2, attn2, wo, gpm, gpf, wg, wu, wd, gpo, tm):
    n, d = x2.shape
    dff = wg.shape[1]
    const = lambda i: (0, 0)
    resident = dict(pipeline_mode=pl.Buffered(1))
    return pl.pallas_call(
        _out_ffn_kernel,
        grid=(n // tm,),
        in_specs=[
            pl.BlockSpec((tm, d), lambda i: (i, 0)),
            pl.BlockSpec((tm, CONV_CH), lambda i: (i, 0)),
            pl.BlockSpec((tm, SB_WIDTH), lambda i: (i, 0)),
            pl.BlockSpec((CONV_CH + SB_WIDTH, d), const, **resident),
            pl.BlockSpec((1, d), const),
            pl.BlockSpec((1, d), const),
            pl.BlockSpec((d, dff), const, **resident),
            pl.BlockSpec((d, dff), const, **resident),
            pl.BlockSpec((dff, d), const, **resident),
            pl.BlockSpec((1, d), const),
        ],
        out_specs=pl.BlockSpec((tm, d), lambda i: (i, 0)),
        out_shape=jax.ShapeDtypeStruct((n, d), F32),
        compiler_params=pltpu.CompilerParams(
            dimension_semantics=("arbitrary",), vmem_limit_bytes=VMEM_LIMIT_BYTES),
        name="out_ffn",
    )(x2, conv2, attn2, wo, gpm, gpf, wg, wu, wd, gpo)


def _layer(h, g_pre_mix, w_in, conv_w, conv_b, conv_ln_g, conv_ln_b, attn_norm_g, w_out,
           g_post_mix, g_pre_ffn, w_gate, w_up, w_down, g_post_ffn):
    b, s, d = h.shape
    n = b * s
    tm = min(512, n)
    x2 = h.reshape(n, d)
    glu, q, k, v = _in_proj(x2, g_pre_mix.reshape(1, d), w_in.astype(BF16), tm)
    conv_out = _conv(glu.reshape(b, s, CONV_CH), conv_w.reshape(CONV_WIDTH, CONV_CH),
                     conv_b.reshape(1, CONV_CH), conv_ln_g.reshape(1, CONV_CH),
                     conv_ln_b.reshape(1, CONV_CH), ts=min(256, s), rc=32)
    attn_out = _attention(q.reshape(b, s, SB_WIDTH), k.reshape(b, s, SB_WIDTH),
                          v.reshape(b, s, SB_WIDTH),
                          attn_norm_g.reshape(N_PAIRS, 1, HEAD_PAIR), t=min(256, s))
    out = _out_ffn(x2, conv_out.reshape(n, CONV_CH), attn_out.reshape(n, SB_WIDTH),
                   w_out.astype(BF16), g_post_mix.reshape(1, d), g_pre_ffn.reshape(1, d),
                   w_gate.astype(BF16), w_up.astype(BF16), w_down.astype(BF16),
                   g_post_ffn.reshape(1, d), tm)
    return out.reshape(b, s, d)


def kernel(x, g_pre_mix, w_in, conv_w, conv_b, conv_ln_g, conv_ln_b, attn_norm_g, w_out,
           g_post_mix, g_pre_ffn, w_gate, w_up, w_down, g_post_ffn):
    h = x
    for l in range(g_pre_mix.shape[0]):
        h = _layer(h, g_pre_mix[l], w_in[l], conv_w[l], conv_b[l], conv_ln_g[l], conv_ln_b[l],
                   attn_norm_g[l], w_out[l], g_post_mix[l], g_pre_ffn[l], w_gate[l], w_up[l],
                   w_down[l], g_post_ffn[l])
    return h
```

```python
import functools
import math

import jax
import jax.numpy as jnp
from jax import lax
from jax.experimental import pallas as pl
from jax.experimental.pallas import tpu as pltpu

EPS = 1e-6
CONV_CH = 512
CONV_WIDTH = 31
SB_HEADS = 8
SB_HEAD_DIM = 64
SB_WIDTH = SB_HEADS * SB_HEAD_DIM
HEAD_PAIR = 2 * SB_HEAD_DIM
N_PAIRS = SB_HEADS // 2
CONV_HALO = 32

VMEM_LIMIT_BYTES = 56 * 1024 * 1024

LOG2_E = 1.4426950408889634

F32 = jnp.float32
BF16 = jnp.bfloat16


def _rms(x, g):
    return x * lax.rsqrt(jnp.mean(x * x, axis=-1, keepdims=True) + EPS) * g


def _in_proj_kernel(x_ref, g_ref, w_ref, glu_ref, q_ref, k_ref, v_ref, *, q_scale):
    a = _rms(x_ref[...], g_ref[...])
    u = jnp.dot(a.astype(BF16), w_ref[...], preferred_element_type=F32)
    val = u[:, :CONV_CH]
    gate = u[:, CONV_CH:2 * CONV_CH]
    glu_ref[...] = val * jax.nn.sigmoid(gate)
    o = 2 * CONV_CH
    q_ref[...] = (u[:, o:o + SB_WIDTH] * q_scale).astype(BF16)
    k_ref[...] = u[:, o + SB_WIDTH:o + 2 * SB_WIDTH].astype(BF16)
    v_ref[...] = u[:, o + 2 * SB_WIDTH:o + 3 * SB_WIDTH].astype(BF16)


def _in_proj(x2, g, w_bf16, tm):
    n, d = x2.shape
    cols = w_bf16.shape[1]
    kern = functools.partial(_in_proj_kernel, q_scale=LOG2_E / math.sqrt(SB_HEAD_DIM))
    return pl.pallas_call(
        kern,
        grid=(n // tm,),
        in_specs=[
            pl.BlockSpec((tm, d), lambda i: (i, 0)),
            pl.BlockSpec((1, d), lambda i: (0, 0)),
            pl.BlockSpec((d, cols), lambda i: (0, 0)),
        ],
        out_specs=[
            pl.BlockSpec((tm, CONV_CH), lambda i: (i, 0)),
            pl.BlockSpec((tm, SB_WIDTH), lambda i: (i, 0)),
            pl.BlockSpec((tm, SB_WIDTH), lambda i: (i, 0)),
            pl.BlockSpec((tm, SB_WIDTH), lambda i: (i, 0)),
        ],
        out_shape=[
            jax.ShapeDtypeStruct((n, CONV_CH), F32),
            jax.ShapeDtypeStruct((n, SB_WIDTH), BF16),
            jax.ShapeDtypeStruct((n, SB_WIDTH), BF16),
            jax.ShapeDtypeStruct((n, SB_WIDTH), BF16),
        ],
        compiler_params=pltpu.CompilerParams(
            dimension_semantics=("arbitrary",), vmem_limit_bytes=VMEM_LIMIT_BYTES),
        name="in_proj",
    )(x2, g, w_bf16)


def _conv_kernel(glu_ref, halo_ref, cw_ref, cb_ref, lg_ref, lb_ref, o_ref, buf_ref, *, ts, rc):
    i = pl.program_id(1)
    buf_ref[0:CONV_HALO, :] = jnp.where(i > 0, halo_ref[0], 0.0)
    buf_ref[CONV_HALO:, :] = glu_ref[0]
    cb = cb_ref[...]
    lg = lg_ref[...]
    lb = lb_ref[...]
    base = CONV_HALO - (CONV_WIDTH - 1)
    for r in range(0, ts, rc):
        acc = jnp.broadcast_to(cb, (rc, CONV_CH))
        for w in range(CONV_WIDTH):
            acc = acc + buf_ref[pl.ds(base + r + w, rc), :] * cw_ref[w:w + 1, :]
        mu = jnp.mean(acc, axis=-1, keepdims=True)
        cen = acc - mu
        var = jnp.mean(cen * cen, axis=-1, keepdims=True)
        y = cen * lax.rsqrt(var + EPS) * lg + lb
        o_ref[0, r:r + rc, :] = (y * jax.nn.sigmoid(y)).astype(o_ref.dtype)


def _conv(glu, cw, cb, lg, lb, ts, rc):
    b, s, c = glu.shape
    kern = functools.partial(_conv_kernel, ts=ts, rc=rc)
    hb = ts // CONV_HALO
    return pl.pallas_call(
        kern,
        grid=(b, s // ts),
        in_specs=[
            pl.BlockSpec((1, ts, c), lambda bi, i: (bi, i, 0)),
            pl.BlockSpec((1, CONV_HALO, c), lambda bi, i: (bi, jnp.maximum(i * hb - 1, 0), 0)),
            pl.BlockSpec((CONV_WIDTH, c), lambda bi, i: (0, 0)),
            pl.BlockSpec((1, c), lambda bi, i: (0, 0)),
            pl.BlockSpec((1, c), lambda bi, i: (0, 0)),
            pl.BlockSpec((1, c), lambda bi, i: (0, 0)),
        ],
        out_specs=pl.BlockSpec((1, ts, c), lambda bi, i: (bi, i, 0)),
        out_shape=jax.ShapeDtypeStruct((b, s, c), BF16),
        scratch_shapes=[pltpu.VMEM((ts + CONV_HALO, c), F32)],
        compiler_params=pltpu.CompilerParams(
            dimension_semantics=("arbitrary", "arbitrary"), vmem_limit_bytes=VMEM_LIMIT_BYTES),
        name="conv",
    )(glu, glu, cw, cb, lg, lb)


def _attn_kernel(q_ref, k_ref, v_ref, tri_ref, g_ref, o_ref, *, tq, tk):
    i = pl.program_id(2)
    q2 = q_ref[0]
    lane = lax.broadcasted_iota(jnp.int32, (1, HEAD_PAIR), 1)
    first = lane < SB_HEAD_DIM
    zero = jnp.zeros((), BF16)
    q_heads = (jnp.where(first, q2, zero), jnp.where(first, zero, q2))
    tri = tri_ref[...]
    n_diag = tq // tk

    def sweep(j, acc, carries, masked):
        start = pl.multiple_of(j * tk, tk)
        kt = k_ref[0, pl.ds(start, tk), :]
        vt = v_ref[0, pl.ds(start, tk), :]
        v_heads = (jnp.where(first, vt, zero), jnp.where(first, zero, vt))
        if masked:
            diff = (lax.broadcasted_iota(jnp.int32, (tq, tk), 1)
                    - lax.broadcasted_iota(jnp.int32, (tq, tk), 0))
            causal = diff < (i * tq - j * tk)
        zs, sps, new_carries = [], [], []
        for h in range(2):
            z = lax.dot_general(q_heads[h], kt, (((1,), (1,)), ((), ())),
                                preferred_element_type=F32)
            sp = jnp.maximum(z, 0.0) + jnp.log(1.0 + jnp.exp2(-jnp.abs(z))) * LOG2_E
            if masked:
                sp = jnp.where(causal, sp, 0.0)
            zs.append(z)
            sps.append(sp.astype(BF16))
            new_carries.append(carries[h] + jnp.sum(sp, axis=1, keepdims=True))
        for h in range(2):
            cs = jnp.dot(sps[h], tri, preferred_element_type=F32)
            p = jnp.exp2(zs[h] - cs - carries[h])
            if masked:
                p = jnp.where(causal, p, 0.0)
            acc = acc + jnp.dot(p.astype(BF16), v_heads[h], preferred_element_type=F32)
        return acc, tuple(new_carries)

    def diag_body(jj, state):
        acc, ca, cb = state
        acc, (ca, cb) = sweep((i + 1) * n_diag - 1 - jj, acc, (ca, cb), True)
        return acc, ca, cb

    def body(jj, state):
        acc, ca, cb = state
        acc, (ca, cb) = sweep(i * n_diag - 1 - jj, acc, (ca, cb), False)
        return acc, ca, cb

    c0 = jnp.zeros((tq, 1), F32)
    state = (jnp.zeros((tq, HEAD_PAIR), F32), c0, c0)
    state = lax.fori_loop(0, n_diag, diag_body, state)
    acc, _, _ = lax.fori_loop(0, i * n_diag, body, state)

    sq = acc * acc
    ms0 = jnp.sum(jnp.where(first, sq, 0.0), axis=-1, keepdims=True) * (1.0 / SB_HEAD_DIM)
    ms1 = jnp.sum(jnp.where(first, 0.0, sq), axis=-1, keepdims=True) * (1.0 / SB_HEAD_DIM)
    inv = jnp.where(first, lax.rsqrt(ms0 + EPS), lax.rsqrt(ms1 + EPS))
    o_ref[0] = (acc * inv * g_ref[0]).astype(o_ref.dtype)


def _attention(q, k, v, g_pairs, tq, tk):
    b, s, _ = q.shape
    tri = (lax.broadcasted_iota(jnp.int32, (tk, tk), 0)
           >= lax.broadcasted_iota(jnp.int32, (tk, tk), 1)).astype(BF16)
    kern = functools.partial(_attn_kernel, tq=tq, tk=tk)
    return pl.pallas_call(
        kern,
        grid=(b, N_PAIRS, s // tq),
        in_specs=[
            pl.BlockSpec((1, tq, HEAD_PAIR), lambda bi, p, i: (bi, i, p)),
            pl.BlockSpec((1, s, HEAD_PAIR), lambda bi, p, i: (bi, 0, p)),
            pl.BlockSpec((1, s, HEAD_PAIR), lambda bi, p, i: (bi, 0, p)),
            pl.BlockSpec((tk, tk), lambda bi, p, i: (0, 0)),
            pl.BlockSpec((1, 1, HEAD_PAIR), lambda bi, p, i: (p, 0, 0)),
        ],
        out_specs=pl.BlockSpec((1, tq, HEAD_PAIR), lambda bi, p, i: (bi, i, p)),
        out_shape=jax.ShapeDtypeStruct((b, s, SB_WIDTH), BF16),
        compiler_params=pltpu.CompilerParams(
            dimension_semantics=("arbitrary", "arbitrary", "arbitrary"),
            vmem_limit_bytes=VMEM_LIMIT_BYTES),
        name="attn",
    )(q, k, v, tri, g_pairs)


def _out_ffn_kernel(x_ref, conv_ref, attn_ref, wo_ref, gpm_ref, gpf_ref, wg_ref, wu_ref, wd_ref,
                    gpo_ref, o_ref):
    y = (jnp.dot(conv_ref[...], wo_ref[0:CONV_CH, :], preferred_element_type=F32)
         + jnp.dot(attn_ref[...], wo_ref[CONV_CH:, :], preferred_element_type=F32))
    h = x_ref[...] + _rms(y, gpm_ref[...])
    f_in = _rms(h, gpf_ref[...]).astype(BF16)
    gt = jnp.dot(f_in, wg_ref[...], preferred_element_type=F32)
    up = jnp.dot(f_in, wu_ref[...], preferred_element_type=F32)
    act = (gt * jax.nn.sigmoid(gt) * up).astype(BF16)
    f = jnp.dot(act, wd_ref[...], preferred_element_type=F32)
    o_ref[...] = h + _rms(f, gpo_ref[...])


def _out_ffn(x2, conv2, attn2, wo, gpm, gpf, wg, wu, wd, gpo, tm):
    n, d = x2.shape
    dff = wg.shape[1]
    const = lambda i: (0, 0)
    resident = dict(pipeline_mode=pl.Buffered(1))
    return pl.pallas_call(
        _out_ffn_kernel,
        grid=(n // tm,),
        in_specs=[
            pl.BlockSpec((tm, d), lambda i: (i, 0)),
            pl.BlockSpec((tm, CONV_CH), lambda i: (i, 0)),
            pl.BlockSpec((tm, SB_WIDTH), lambda i: (i, 0)),
            pl.BlockSpec((CONV_CH + SB_WIDTH, d), const, **resident),
            pl.BlockSpec((1, d), const),
            pl.BlockSpec((1, d), const),
            pl.BlockSpec((d, dff), const, **resident),
            pl.BlockSpec((d, dff), const, **resident),
            pl.BlockSpec((dff, d), const, **resident),
            pl.BlockSpec((1, d), const),
        ],
        out_specs=pl.BlockSpec((tm, d), lambda i: (i, 0)),
        out_shape=jax.ShapeDtypeStruct((n, d), F32),
        compiler_params=pltpu.CompilerParams(
            dimension_semantics=("arbitrary",), vmem_limit_bytes=VMEM_LIMIT_BYTES),
        name="out_ffn",
    )(x2, conv2, attn2, wo, gpm, gpf, wg, wu, wd, gpo)


def _layer(h, g_pre_mix, w_in, conv_w, conv_b, conv_ln_g, conv_ln_b, attn_norm_g, w_out,
           g_post_mix, g_pre_ffn, w_gate, w_up, w_down, g_post_ffn):
    b, s, d = h.shape
    n = b * s
    tm = min(512, n)
    x2 = h.reshape(n, d)
    glu, q, k, v = _in_proj(x2, g_pre_mix.reshape(1, d), w_in.astype(BF16), tm)
    conv_out = _conv(glu.reshape(b, s, CONV_CH), conv_w.reshape(CONV_WIDTH, CONV_CH),
                     conv_b.reshape(1, CONV_CH), conv_ln_g.reshape(1, CONV_CH),
                     conv_ln_b.reshape(1, CONV_CH), ts=min(256, s), rc=32)
    attn_out = _attention(q.reshape(b, s, SB_WIDTH), k.reshape(b, s, SB_WIDTH),
                          v.reshape(b, s, SB_WIDTH),
                          attn_norm_g.reshape(N_PAIRS, 1, HEAD_PAIR), tq=min(256, s), tk=min(256, s))
    out = _out_ffn(x2, conv_out.reshape(n, CONV_CH), attn_out.reshape(n, SB_WIDTH),
                   w_out.astype(BF16), g_post_mix.reshape(1, d), g_pre_ffn.reshape(1, d),
                   w_gate.astype(BF16), w_up.astype(BF16), w_down.astype(BF16),
                   g_post_ffn.reshape(1, d), tm)
    return out.reshape(b, s, d)


def kernel(x, g_pre_mix, w_in, conv_w, conv_b, conv_ln_g, conv_ln_b, attn_norm_g, w_out,
           g_post_mix, g_pre_ffn, w_gate, w_up, w_down, g_post_ffn):
    h = x
    for l in range(g_pre_mix.shape[0]):
        h = _layer(h, g_pre_mix[l], w_in[l], conv_w[l], conv_b[l], conv_ln_g[l], conv_ln_b[l],
                   attn_norm_g[l], w_out[l], g_post_mix[l], g_pre_ffn[l], w_gate[l], w_up[l],
                   w_down[l], g_post_ffn[l])
    return h
```

```python
import functools
import math

import jax
import jax.numpy as jnp
from jax import lax
from jax.experimental import pallas as pl
from jax.experimental.pallas import tpu as pltpu

EPS = 1e-6
CONV_CH = 512
CONV_WIDTH = 31
SB_HEADS = 8
SB_HEAD_DIM = 64
SB_WIDTH = SB_HEADS * SB_HEAD_DIM
ATTN_HEADS_PER_STEP = 4
ATTN_TILE = 256
MASKED_LOGIT = -1e30
CONV_HALO = 32
SUBLANES = 8

VMEM_LIMIT_BYTES = 56 * 1024 * 1024

LOG2_E = 1.4426950408889634

F32 = jnp.float32
BF16 = jnp.bfloat16


def _rms(x, g):
    return x * lax.rsqrt(jnp.mean(x * x, axis=-1, keepdims=True) + EPS) * g


def _in_proj_kernel(x_ref, g_ref, w_ref, glu_ref, q_ref, k_ref, v_ref, *, q_scale):
    a = _rms(x_ref[...], g_ref[...])
    u = jnp.dot(a.astype(BF16), w_ref[...], preferred_element_type=F32)
    val = u[:, :CONV_CH]
    gate = u[:, CONV_CH:2 * CONV_CH]
    glu_ref[...] = val * jax.nn.sigmoid(gate)
    o = 2 * CONV_CH
    q_ref[...] = (u[:, o:o + SB_WIDTH] * q_scale).astype(BF16)
    k_ref[...] = u[:, o + SB_WIDTH:o + 2 * SB_WIDTH].astype(BF16)
    v_ref[...] = u[:, o + 2 * SB_WIDTH:o + 3 * SB_WIDTH].astype(BF16)


def _in_proj(x2, g, w_bf16, tm):
    n, d = x2.shape
    cols = w_bf16.shape[1]
    kern = functools.partial(_in_proj_kernel, q_scale=LOG2_E / math.sqrt(SB_HEAD_DIM))
    return pl.pallas_call(
        kern,
        grid=(n // tm,),
        in_specs=[
            pl.BlockSpec((tm, d), lambda i: (i, 0)),
            pl.BlockSpec((1, d), lambda i: (0, 0)),
            pl.BlockSpec((d, cols), lambda i: (0, 0)),
        ],
        out_specs=[
            pl.BlockSpec((tm, CONV_CH), lambda i: (i, 0)),
            pl.BlockSpec((tm, SB_WIDTH), lambda i: (i, 0)),
            pl.BlockSpec((tm, SB_WIDTH), lambda i: (i, 0)),
            pl.BlockSpec((tm, SB_WIDTH), lambda i: (i, 0)),
        ],
        out_shape=[
            jax.ShapeDtypeStruct((n, CONV_CH), F32),
            jax.ShapeDtypeStruct((n, SB_WIDTH), BF16),
            jax.ShapeDtypeStruct((n, SB_WIDTH), BF16),
            jax.ShapeDtypeStruct((n, SB_WIDTH), BF16),
        ],
        compiler_params=pltpu.CompilerParams(
            dimension_semantics=("arbitrary",), vmem_limit_bytes=VMEM_LIMIT_BYTES),
        name="in_proj",
    )(x2, g, w_bf16)


def _conv_kernel(glu_ref, halo_ref, cw_ref, cb_ref, lg_ref, lb_ref, o_ref, buf_ref, sh_ref, *,
                 ts, rc):
    i = pl.program_id(1)
    buf_ref[0:CONV_HALO, :] = jnp.where(i > 0, halo_ref[0], 0.0)
    buf_ref[CONV_HALO:, :] = glu_ref[0]
    cb = cb_ref[...]
    lg = lg_ref[...]
    lb = lb_ref[...]
    base = CONV_HALO - (CONV_WIDTH - 1)
    for b in range(SUBLANES):
        span = ts + SUBLANES * ((CONV_WIDTH - 1 - b) // SUBLANES)
        sh_ref[b, 0:span // SUBLANES] = buf_ref[pl.ds(base + b, span), :].reshape(
            span // SUBLANES, SUBLANES, CONV_CH)
    tr, tc = ts // SUBLANES, rc // SUBLANES
    for r in range(0, tr, tc):
        acc = jnp.broadcast_to(cb, (tc, SUBLANES, CONV_CH))
        for w in range(CONV_WIDTH):
            a, b = divmod(w, SUBLANES)
            acc = acc + sh_ref[b, r + a:r + a + tc] * cw_ref[w]
        mu = jnp.mean(acc, axis=-1, keepdims=True)
        cen = acc - mu
        var = jnp.mean(cen * cen, axis=-1, keepdims=True)
        y = cen * lax.rsqrt(var + EPS) * lg + lb
        o_ref[0, r * SUBLANES:(r + tc) * SUBLANES, :] = (
            (y * jax.nn.sigmoid(y)).reshape(rc, CONV_CH).astype(o_ref.dtype))


def _conv(glu, cw, cb, lg, lb, ts, rc):
    b, s, c = glu.shape
    kern = functools.partial(_conv_kernel, ts=ts, rc=rc)
    hb = ts // CONV_HALO
    return pl.pallas_call(
        kern,
        grid=(b, s // ts),
        in_specs=[
            pl.BlockSpec((1, ts, c), lambda bi, i: (bi, i, 0)),
            pl.BlockSpec((1, CONV_HALO, c), lambda bi, i: (bi, jnp.maximum(i * hb - 1, 0), 0)),
            pl.BlockSpec((CONV_WIDTH, SUBLANES, c), lambda bi, i: (0, 0, 0)),
            pl.BlockSpec((1, c), lambda bi, i: (0, 0)),
            pl.BlockSpec((1, c), lambda bi, i: (0, 0)),
            pl.BlockSpec((1, c), lambda bi, i: (0, 0)),
        ],
        out_specs=pl.BlockSpec((1, ts, c), lambda bi, i: (bi, i, 0)),
        out_shape=jax.ShapeDtypeStruct((b, s, c), BF16),
        scratch_shapes=[
            pltpu.VMEM((ts + CONV_HALO, c), F32),
            pltpu.VMEM((SUBLANES, ts // SUBLANES + (CONV_WIDTH - 1) // SUBLANES, SUBLANES, c), F32),
        ],
        compiler_params=pltpu.CompilerParams(
            dimension_semantics=("arbitrary", "arbitrary"), vmem_limit_bytes=VMEM_LIMIT_BYTES),
        name="conv",
    )(glu, glu, jnp.broadcast_to(cw[:, None, :], (CONV_WIDTH, SUBLANES, c)), cb, lg, lb)


def _attn_kernel(q_ref, k_ref, v_ref, tri_ref, g_ref, o_ref, z_a, z_b, zc_buf, p_buf, *, t, nh):
    i = pl.program_id(2)
    n_steps = i + 1
    qb = q_ref[0]
    lane = lax.broadcasted_iota(jnp.int32, (1, nh * SB_HEAD_DIM), 1)
    zero = jnp.zeros((), BF16)
    in_head = [(lane >= h * SB_HEAD_DIM) & (lane < (h + 1) * SB_HEAD_DIM) for h in range(nh)]
    q_heads = [jnp.where(m, qb, zero) for m in in_head]
    tri = tri_ref[...]

    def key_tile(ref, n):
        j = jnp.maximum(i - n, 0)
        return ref[0, pl.ds(pl.multiple_of(j * t, t), t), :]

    def output(n, acc):
        vt = key_tile(v_ref, n)
        for h in range(nh):
            acc = acc + jnp.dot(p_buf[h], jnp.where(in_head[h], vt, zero),
                                preferred_element_type=F32)
        return acc

    def logits(n, z_out):
        kt = key_tile(k_ref, n)
        for h in range(nh):
            z_out[h] = lax.dot_general(q_heads[h], kt, (((1,), (1,)), ((), ())),
                                       preferred_element_type=F32)

    def weights(z_in, carries, causal=None):
        css, new_carries = [], []
        for h in range(nh):
            z = z_in[h]
            sp = jnp.maximum(z, 0.0) + jnp.log(1.0 + jnp.exp2(-jnp.abs(z))) * LOG2_E
            if causal is not None:
                sp = jnp.where(causal, sp, 0.0)
                z = jnp.where(causal, z, MASKED_LOGIT)
            zc_buf[h] = z - carries[h]
            css.append(jnp.dot(sp.astype(BF16), tri, preferred_element_type=F32))
            new_carries.append(carries[h] + jnp.sum(sp, axis=1, keepdims=True))
        for h in range(nh):
            p_buf[h] = jnp.exp2(zc_buf[h] - css[h]).astype(BF16)
        return tuple(new_carries)

    causal = (lax.broadcasted_iota(jnp.int32, (t, t), 1)
              < lax.broadcasted_iota(jnp.int32, (t, t), 0))
    c0 = jnp.zeros((t, 1), F32)
    logits(0, z_a)
    logits(1, z_b)
    carries = weights(z_a, (c0,) * nh, causal)

    def step_pair(u, state):
        acc, carries = state
        m = 2 + 2 * u
        acc = output(m - 2, acc)
        logits(m, z_a)
        carries = weights(z_b, carries)
        acc = output(m - 1, acc)
        logits(m + 1, z_b)
        carries = weights(z_a, carries)
        return acc, carries

    acc = jnp.zeros((t, nh * SB_HEAD_DIM), F32)
    acc, _ = lax.fori_loop(0, n_steps // 2, step_pair, (acc, carries))
    acc = lax.cond(n_steps % 2 == 1, lambda a: output(n_steps - 1, a), lambda a: a, acc)

    sq = acc * acc
    inv = jnp.zeros_like(acc)
    for h in range(nh):
        ms = jnp.sum(jnp.where(in_head[h], sq, 0.0), axis=-1, keepdims=True) * (1.0 / SB_HEAD_DIM)
        inv = jnp.where(in_head[h], lax.rsqrt(ms + EPS), inv)
    o_ref[0] = (acc * inv * g_ref[0]).astype(o_ref.dtype)


def _attention(q, k, v, g_groups, t, nh):
    b, s, _ = q.shape
    w = nh * SB_HEAD_DIM
    tri = (lax.broadcasted_iota(jnp.int32, (t, t), 0)
           >= lax.broadcasted_iota(jnp.int32, (t, t), 1)).astype(BF16)
    kern = functools.partial(_attn_kernel, t=t, nh=nh)
    return pl.pallas_call(
        kern,
        grid=(b, SB_HEADS // nh, s // t),
        in_specs=[
            pl.BlockSpec((1, t, w), lambda bi, p, i: (bi, i, p)),
            pl.BlockSpec((1, s, w), lambda bi, p, i: (bi, 0, p)),
            pl.BlockSpec((1, s, w), lambda bi, p, i: (bi, 0, p)),
            pl.BlockSpec((t, t), lambda bi, p, i: (0, 0)),
            pl.BlockSpec((1, 1, w), lambda bi, p, i: (p, 0, 0)),
        ],
        out_specs=pl.BlockSpec((1, t, w), lambda bi, p, i: (bi, i, p)),
        out_shape=jax.ShapeDtypeStruct((b, s, SB_WIDTH), BF16),
        scratch_shapes=[pltpu.VMEM((nh, t, t), F32), pltpu.VMEM((nh, t, t), F32),
                        pltpu.VMEM((nh, t, t), F32), pltpu.VMEM((nh, t, t), BF16)],
        compiler_params=pltpu.CompilerParams(
            dimension_semantics=("arbitrary", "arbitrary", "arbitrary"),
            vmem_limit_bytes=VMEM_LIMIT_BYTES),
        name="attn",
    )(q, k, v, tri, g_groups)


def _out_ffn_kernel(x_ref, conv_ref, attn_ref, wo_ref, gpm_ref, gpf_ref, wg_ref, wu_ref, wd_ref,
                    gpo_ref, o_ref):
    y = (jnp.dot(conv_ref[...], wo_ref[0:CONV_CH, :], preferred_element_type=F32)
         + jnp.dot(attn_ref[...], wo_ref[CONV_CH:, :], preferred_element_type=F32))
    h = x_ref[...] + _rms(y, gpm_ref[...])
    f_in = _rms(h, gpf_ref[...]).astype(BF16)
    gt = jnp.dot(f_in, wg_ref[...], preferred_element_type=F32)
    up = jnp.dot(f_in, wu_ref[...], preferred_element_type=F32)
    act = (gt * jax.nn.sigmoid(gt) * up).astype(BF16)
    f = jnp.dot(act, wd_ref[...], preferred_element_type=F32)
    o_ref[...] = h + _rms(f, gpo_ref[...])


def _out_ffn(x2, conv2, attn2, wo, gpm, gpf, wg, wu, wd, gpo, tm):
    n, d = x2.shape
    dff = wg.shape[1]
    const = lambda i: (0, 0)
    resident = dict(pipeline_mode=pl.Buffered(1))
    return pl.pallas_call(
        _out_ffn_kernel,
        grid=(n // tm,),
        in_specs=[
            pl.BlockSpec((tm, d), lambda i: (i, 0)),
            pl.BlockSpec((tm, CONV_CH), lambda i: (i, 0)),
            pl.BlockSpec((tm, SB_WIDTH), lambda i: (i, 0)),
            pl.BlockSpec((CONV_CH + SB_WIDTH, d), const, **resident),
            pl.BlockSpec((1, d), const),
            pl.BlockSpec((1, d), const),
            pl.BlockSpec((d, dff), const, **resident),
            pl.BlockSpec((d, dff), const, **resident),
            pl.BlockSpec((dff, d), const, **resident),
            pl.BlockSpec((1, d), const),
        ],
        out_specs=pl.BlockSpec((tm, d), lambda i: (i, 0)),
        out_shape=jax.ShapeDtypeStruct((n, d), F32),
        compiler_params=pltpu.CompilerParams(
            dimension_semantics=("arbitrary",), vmem_limit_bytes=VMEM_LIMIT_BYTES),
        name="out_ffn",
    )(x2, conv2, attn2, wo, gpm, gpf, wg, wu, wd, gpo)


def _layer(h, g_pre_mix, w_in, conv_w, conv_b, conv_ln_g, conv_ln_b, attn_norm_g, w_out,
           g_post_mix, g_pre_ffn, w_gate, w_up, w_down, g_post_ffn):
    b, s, d = h.shape
    n = b * s
    tm = min(512, n)
    x2 = h.reshape(n, d)
    glu, q, k, v = _in_proj(x2, g_pre_mix.reshape(1, d), w_in.astype(BF16), tm)
    conv_out = _conv(glu.reshape(b, s, CONV_CH), conv_w.reshape(CONV_WIDTH, CONV_CH),
                     conv_b.reshape(1, CONV_CH), conv_ln_g.reshape(1, CONV_CH),
                     conv_ln_b.reshape(1, CONV_CH), ts=min(256, s), rc=32)
    nh = ATTN_HEADS_PER_STEP
    attn_out = _attention(q.reshape(b, s, SB_WIDTH), k.reshape(b, s, SB_WIDTH),
                          v.reshape(b, s, SB_WIDTH),
                          attn_norm_g.reshape(SB_HEADS // nh, 1, nh * SB_HEAD_DIM),
                          t=min(ATTN_TILE, s), nh=nh)
    out = _out_ffn(x2, conv_out.reshape(n, CONV_CH), attn_out.reshape(n, SB_WIDTH),
                   w_out.astype(BF16), g_post_mix.reshape(1, d), g_pre_ffn.reshape(1, d),
                   w_gate.astype(BF16), w_up.astype(BF16), w_down.astype(BF16),
                   g_post_ffn.reshape(1, d), tm)
    return out.reshape(b, s, d)


def kernel(x, g_pre_mix, w_in, conv_w, conv_b, conv_ln_g, conv_ln_b, attn_norm_g, w_out,
           g_post_mix, g_pre_ffn, w_gate, w_up, w_down, g_post_ffn):
    h = x
    for l in range(g_pre_mix.shape[0]):
        h = _layer(h, g_pre_mix[l], w_in[l], conv_w[l], conv_b[l], conv_ln_g[l], conv_ln_b[l],
                   attn_norm_g[l], w_out[l], g_post_mix[l], g_pre_ffn[l], w_gate[l], w_up[l],
                   w_down[l], g_post_ffn[l])
    return h
```

```python
import functools
import math

import jax
import jax.numpy as jnp
from jax import lax
from jax.experimental import pallas as pl
from jax.experimental.pallas import tpu as pltpu

EPS = 1e-6
CONV_CH = 512
CONV_WIDTH = 31
SB_HEADS = 8
SB_HEAD_DIM = 64
SB_WIDTH = SB_HEADS * SB_HEAD_DIM
ATTN_HEADS_PER_STEP = 4
ATTN_TILE = 256
MASKED_LOGIT = -1e30
SWEEP_STOP_LOG2 = 160.0
CONV_HALO = 32
SUBLANES = 8

VMEM_LIMIT_BYTES = 56 * 1024 * 1024

LOG2_E = 1.4426950408889634

F32 = jnp.float32
BF16 = jnp.bfloat16


def _rms(x, g):
    return x * lax.rsqrt(jnp.mean(x * x, axis=-1, keepdims=True) + EPS) * g


def _in_proj_kernel(x_ref, g_ref, w_ref, glu_ref, q_ref, k_ref, v_ref, *, q_scale):
    a = _rms(x_ref[...], g_ref[...])
    u = jnp.dot(a.astype(BF16), w_ref[...], preferred_element_type=F32)
    val = u[:, :CONV_CH]
    gate = u[:, CONV_CH:2 * CONV_CH]
    glu_ref[...] = val * jax.nn.sigmoid(gate)
    o = 2 * CONV_CH
    q_ref[...] = (u[:, o:o + SB_WIDTH] * q_scale).astype(BF16)
    k_ref[...] = u[:, o + SB_WIDTH:o + 2 * SB_WIDTH].astype(BF16)
    v_ref[...] = u[:, o + 2 * SB_WIDTH:o + 3 * SB_WIDTH].astype(BF16)


def _in_proj(x2, g, w_bf16, tm):
    n, d = x2.shape
    cols = w_bf16.shape[1]
    kern = functools.partial(_in_proj_kernel, q_scale=LOG2_E / math.sqrt(SB_HEAD_DIM))
    return pl.pallas_call(
        kern,
        grid=(n // tm,),
        in_specs=[
            pl.BlockSpec((tm, d), lambda i: (i, 0)),
            pl.BlockSpec((1, d), lambda i: (0, 0)),
            pl.BlockSpec((d, cols), lambda i: (0, 0)),
        ],
        out_specs=[
            pl.BlockSpec((tm, CONV_CH), lambda i: (i, 0)),
            pl.BlockSpec((tm, SB_WIDTH), lambda i: (i, 0)),
            pl.BlockSpec((tm, SB_WIDTH), lambda i: (i, 0)),
            pl.BlockSpec((tm, SB_WIDTH), lambda i: (i, 0)),
        ],
        out_shape=[
            jax.ShapeDtypeStruct((n, CONV_CH), F32),
            jax.ShapeDtypeStruct((n, SB_WIDTH), BF16),
            jax.ShapeDtypeStruct((n, SB_WIDTH), BF16),
            jax.ShapeDtypeStruct((n, SB_WIDTH), BF16),
        ],
        compiler_params=pltpu.CompilerParams(
            dimension_semantics=("arbitrary",), vmem_limit_bytes=VMEM_LIMIT_BYTES),
        name="in_proj",
    )(x2, g, w_bf16)


def _conv_kernel(glu_ref, halo_ref, cw_ref, cb_ref, lg_ref, lb_ref, o_ref, buf_ref, sh_ref, *,
                 ts, rc):
    i = pl.program_id(1)
    buf_ref[0:CONV_HALO, :] = jnp.where(i > 0, halo_ref[0], 0.0)
    buf_ref[CONV_HALO:, :] = glu_ref[0]
    cb = cb_ref[...]
    lg = lg_ref[...]
    lb = lb_ref[...]
    base = CONV_HALO - (CONV_WIDTH - 1)
    for b in range(SUBLANES):
        span = ts + SUBLANES * ((CONV_WIDTH - 1 - b) // SUBLANES)
        sh_ref[b, 0:span // SUBLANES] = buf_ref[pl.ds(base + b, span), :].reshape(
            span // SUBLANES, SUBLANES, CONV_CH)
    tr, tc = ts // SUBLANES, rc // SUBLANES
    for r in range(0, tr, tc):
        acc = jnp.broadcast_to(cb, (tc, SUBLANES, CONV_CH))
        for w in range(CONV_WIDTH):
            a, b = divmod(w, SUBLANES)
            acc = acc + sh_ref[b, r + a:r + a + tc] * cw_ref[w]
        mu = jnp.mean(acc, axis=-1, keepdims=True)
        cen = acc - mu
        var = jnp.mean(cen * cen, axis=-1, keepdims=True)
        y = cen * lax.rsqrt(var + EPS) * lg + lb
        o_ref[0, r * SUBLANES:(r + tc) * SUBLANES, :] = (
            (y * jax.nn.sigmoid(y)).reshape(rc, CONV_CH).astype(o_ref.dtype))


def _conv(glu, cw, cb, lg, lb, ts, rc):
    b, s, c = glu.shape
    kern = functools.partial(_conv_kernel, ts=ts, rc=rc)
    hb = ts // CONV_HALO
    return pl.pallas_call(
        kern,
        grid=(b, s // ts),
        in_specs=[
            pl.BlockSpec((1, ts, c), lambda bi, i: (bi, i, 0)),
            pl.BlockSpec((1, CONV_HALO, c), lambda bi, i: (bi, jnp.maximum(i * hb - 1, 0), 0)),
            pl.BlockSpec((CONV_WIDTH, SUBLANES, c), lambda bi, i: (0, 0, 0)),
            pl.BlockSpec((1, c), lambda bi, i: (0, 0)),
            pl.BlockSpec((1, c), lambda bi, i: (0, 0)),
            pl.BlockSpec((1, c), lambda bi, i: (0, 0)),
        ],
        out_specs=pl.BlockSpec((1, ts, c), lambda bi, i: (bi, i, 0)),
        out_shape=jax.ShapeDtypeStruct((b, s, c), BF16),
        scratch_shapes=[
            pltpu.VMEM((ts + CONV_HALO, c), F32),
            pltpu.VMEM((SUBLANES, ts // SUBLANES + (CONV_WIDTH - 1) // SUBLANES, SUBLANES, c), F32),
        ],
        compiler_params=pltpu.CompilerParams(
            dimension_semantics=("arbitrary", "arbitrary"), vmem_limit_bytes=VMEM_LIMIT_BYTES),
        name="conv",
    )(glu, glu, jnp.broadcast_to(cw[:, None, :], (CONV_WIDTH, SUBLANES, c)), cb, lg, lb)


def _attn_kernel(q_ref, k_ref, v_ref, tri_ref, g_ref, o_ref, *, t, nh):
    i = pl.program_id(2)
    qb = q_ref[0]
    lane = lax.broadcasted_iota(jnp.int32, (1, nh * SB_HEAD_DIM), 1)
    zero = jnp.zeros((), BF16)
    in_head = [(lane >= h * SB_HEAD_DIM) & (lane < (h + 1) * SB_HEAD_DIM) for h in range(nh)]
    q_heads = [jnp.where(m, qb, zero) for m in in_head]
    tri = tri_ref[...]

    def key_tile(ref, n):
        j = jnp.maximum(i - n, 0)
        return ref[0, pl.ds(pl.multiple_of(j * t, t), t), :]

    def scores(n, causal=None):
        kt = key_tile(k_ref, n)
        zs, sps, tots = [], [], []
        for h in range(nh):
            z = lax.dot_general(q_heads[h], kt, (((1,), (1,)), ((), ())),
                                preferred_element_type=F32)
            sp = jnp.maximum(z, 0.0) + jnp.log(1.0 + jnp.exp2(-jnp.abs(z))) * LOG2_E
            if causal is not None:
                sp = jnp.where(causal, sp, 0.0)
                z = jnp.where(causal, z, MASKED_LOGIT)
            zs.append(z)
            sps.append(sp.astype(BF16))
            tots.append(jnp.sum(sp, axis=1, keepdims=True))
        return zs, sps, tots

    def weighted_values(n, zs, sps, carries, acc):
        vt = key_tile(v_ref, n)
        for h in range(nh):
            cs = jnp.dot(sps[h], tri, preferred_element_type=F32)
            p = jnp.exp2(zs[h] - cs - carries[h])
            acc = acc + jnp.dot(p.astype(BF16), jnp.where(in_head[h], vt, zero),
                                preferred_element_type=F32)
        return acc

    def min_carry(carries):
        return jnp.min(functools.reduce(jnp.minimum, carries))

    causal = (lax.broadcasted_iota(jnp.int32, (t, t), 1)
              < lax.broadcasted_iota(jnp.int32, (t, t), 0))
    zs0, sps0, tots0 = scores(0, causal)
    zs1, sps1, tots1 = scores(1)
    no_step1 = jnp.where(i == 0, jnp.inf, 0.0).astype(F32)
    c0 = jnp.zeros((t, 1), F32)
    acc = jnp.zeros((t, nh * SB_HEAD_DIM), F32)
    acc = weighted_values(0, zs0, sps0, (c0,) * nh, acc)
    acc = weighted_values(1, zs1, sps1, tuple(s + no_step1 for s in tots0), acc)
    carries = tuple(a + b for a, b in zip(tots0, tots1))

    def more(state):
        n, _, _, low = state
        return (n <= i) & (low < SWEEP_STOP_LOG2)

    def step(state):
        n, acc, carries, _ = state
        zs, sps, tots = scores(n)
        acc = weighted_values(n, zs, sps, carries, acc)
        carries = tuple(c + s for c, s in zip(carries, tots))
        return n + 1, acc, carries, min_carry(carries)

    _, acc, _, _ = lax.while_loop(more, step, (jnp.int32(2), acc, carries, min_carry(carries)))

    sq = acc * acc
    inv = jnp.zeros_like(acc)
    for h in range(nh):
        ms = jnp.sum(jnp.where(in_head[h], sq, 0.0), axis=-1, keepdims=True) * (1.0 / SB_HEAD_DIM)
        inv = jnp.where(in_head[h], lax.rsqrt(ms + EPS), inv)
    o_ref[0] = (acc * inv * g_ref[0]).astype(o_ref.dtype)


def _attention(q, k, v, g_groups, t, nh):
    b, s, _ = q.shape
    w = nh * SB_HEAD_DIM
    tri = (lax.broadcasted_iota(jnp.int32, (t, t), 0)
           >= lax.broadcasted_iota(jnp.int32, (t, t), 1)).astype(BF16)
    kern = functools.partial(_attn_kernel, t=t, nh=nh)
    return pl.pallas_call(
        kern,
        grid=(b, SB_HEADS // nh, s // t),
        in_specs=[
            pl.BlockSpec((1, t, w), lambda bi, p, i: (bi, i, p)),
            pl.BlockSpec((1, s, w), lambda bi, p, i: (bi, 0, p)),
            pl.BlockSpec((1, s, w), lambda bi, p, i: (bi, 0, p)),
            pl.BlockSpec((t, t), lambda bi, p, i: (0, 0)),
            pl.BlockSpec((1, 1, w), lambda bi, p, i: (p, 0, 0)),
        ],
        out_specs=pl.BlockSpec((1, t, w), lambda bi, p, i: (bi, i, p)),
        out_shape=jax.ShapeDtypeStruct((b, s, SB_WIDTH), BF16),
        compiler_params=pltpu.CompilerParams(
            dimension_semantics=("arbitrary", "arbitrary", "arbitrary"),
            vmem_limit_bytes=VMEM_LIMIT_BYTES),
        name="attn",
    )(q, k, v, tri, g_groups)


def _out_ffn_kernel(x_ref, conv_ref, attn_ref, wo_ref, gpm_ref, gpf_ref, wg_ref, wu_ref, wd_ref,
                    gpo_ref, o_ref):
    y = (jnp.dot(conv_ref[...], wo_ref[0:CONV_CH, :], preferred_element_type=F32)
         + jnp.dot(attn_ref[...], wo_ref[CONV_CH:, :], preferred_element_type=F32))
    h = x_ref[...] + _rms(y, gpm_ref[...])
    f_in = _rms(h, gpf_ref[...]).astype(BF16)
    gt = jnp.dot(f_in, wg_ref[...], preferred_element_type=F32)
    up = jnp.dot(f_in, wu_ref[...], preferred_element_type=F32)
    act = (gt * jax.nn.sigmoid(gt) * up).astype(BF16)
    f = jnp.dot(act, wd_ref[...], preferred_element_type=F32)
    o_ref[...] = h + _rms(f, gpo_ref[...])


def _out_ffn(x2, conv2, attn2, wo, gpm, gpf, wg, wu, wd, gpo, tm):
    n, d = x2.shape
    dff = wg.shape[1]
    const = lambda i: (0, 0)
    resident = dict(pipeline_mode=pl.Buffered(1))
    return pl.pallas_call(
        _out_ffn_kernel,
        grid=(n // tm,),
        in_specs=[
            pl.BlockSpec((tm, d), lambda i: (i, 0)),
            pl.BlockSpec((tm, CONV_CH), lambda i: (i, 0)),
            pl.BlockSpec((tm, SB_WIDTH), lambda i: (i, 0)),
            pl.BlockSpec((CONV_CH + SB_WIDTH, d), const, **resident),
            pl.BlockSpec((1, d), const),
            pl.BlockSpec((1, d), const),
            pl.BlockSpec((d, dff), const, **resident),
            pl.BlockSpec((d, dff), const, **resident),
            pl.BlockSpec((dff, d), const, **resident),
            pl.BlockSpec((1, d), const),
        ],
        out_specs=pl.BlockSpec((tm, d), lambda i: (i, 0)),
        out_shape=jax.ShapeDtypeStruct((n, d), F32),
        compiler_params=pltpu.CompilerParams(
            dimension_semantics=("arbitrary",), vmem_limit_bytes=VMEM_LIMIT_BYTES),
        name="out_ffn",
    )(x2, conv2, attn2, wo, gpm, gpf, wg, wu, wd, gpo)


def _layer(h, g_pre_mix, w_in, conv_w, conv_b, conv_ln_g, conv_ln_b, attn_norm_g, w_out,
           g_post_mix, g_pre_ffn, w_gate, w_up, w_down, g_post_ffn):
    b, s, d = h.shape
    n = b * s
    tm = min(512, n)
    x2 = h.reshape(n, d)
    glu, q, k, v = _in_proj(x2, g_pre_mix.reshape(1, d), w_in.astype(BF16), tm)
    conv_out = _conv(glu.reshape(b, s, CONV_CH), conv_w.reshape(CONV_WIDTH, CONV_CH),
                     conv_b.reshape(1, CONV_CH), conv_ln_g.reshape(1, CONV_CH),
                     conv_ln_b.reshape(1, CONV_CH), ts=min(256, s), rc=32)
    nh = ATTN_HEADS_PER_STEP
    attn_out = _attention(q.reshape(b, s, SB_WIDTH), k.reshape(b, s, SB_WIDTH),
                          v.reshape(b, s, SB_WIDTH),
                          attn_norm_g.reshape(SB_HEADS // nh, 1, nh * SB_HEAD_DIM),
                          t=min(ATTN_TILE, s), nh=nh)
    out = _out_ffn(x2, conv_out.reshape(n, CONV_CH), attn_out.reshape(n, SB_WIDTH),
                   w_out.astype(BF16), g_post_mix.reshape(1, d), g_pre_ffn.reshape(1, d),
                   w_gate.astype(BF16), w_up.astype(BF16), w_down.astype(BF16),
                   g_post_ffn.reshape(1, d), tm)
    return out.reshape(b, s, d)


def kernel(x, g_pre_mix, w_in, conv_w, conv_b, conv_ln_g, conv_ln_b, attn_norm_g, w_out,
           g_post_mix, g_pre_ffn, w_gate, w_up, w_down, g_post_ffn):
    h = x
    for l in range(g_pre_mix.shape[0]):
        h = _layer(h, g_pre_mix[l], w_in[l], conv_w[l], conv_b[l], conv_ln_g[l], conv_ln_b[l],
                   attn_norm_g[l], w_out[l], g_post_mix[l], g_pre_ffn[l], w_gate[l], w_up[l],
                   w_down[l], g_post_ffn[l])
    return h
```

```python
import functools
import math

import jax
import jax.numpy as jnp
from jax import lax
from jax.experimental import pallas as pl
from jax.experimental.pallas import tpu as pltpu

EPS = 1e-6
CONV_CH = 512
CONV_WIDTH = 31
SB_HEADS = 8
SB_HEAD_DIM = 64
SB_WIDTH = SB_HEADS * SB_HEAD_DIM
ATTN_HEADS_PER_STEP = 4
ATTN_TILE = 256
IN_PROJ_ROWS = 1024
OUT_FFN_ROWS = 512
CONV_ROWS = 512
CONV_CHUNK = 32
MASKED_LOGIT = -1e30
SWEEP_STOP_LOG2 = 160.0
CONV_HALO = 32
SUBLANES = 8

VMEM_LIMIT_BYTES = 56 * 1024 * 1024

LOG2_E = 1.4426950408889634

F32 = jnp.float32
BF16 = jnp.bfloat16


def _rms(x, g):
    return x * lax.rsqrt(jnp.mean(x * x, axis=-1, keepdims=True) + EPS) * g


def _in_proj_kernel(x_ref, g_ref, w_ref, glu_ref, q_ref, k_ref, v_ref, *, q_scale):
    a = _rms(x_ref[...], g_ref[...])
    u = jnp.dot(a.astype(BF16), w_ref[...], preferred_element_type=F32)
    val = u[:, :CONV_CH]
    gate = u[:, CONV_CH:2 * CONV_CH]
    glu_ref[...] = val * jax.nn.sigmoid(gate)
    o = 2 * CONV_CH
    q_ref[...] = (u[:, o:o + SB_WIDTH] * q_scale).astype(BF16)
    k_ref[...] = u[:, o + SB_WIDTH:o + 2 * SB_WIDTH].astype(BF16)
    v_ref[...] = u[:, o + 2 * SB_WIDTH:o + 3 * SB_WIDTH].astype(BF16)


def _in_proj(x2, g, w_bf16, tm):
    n, d = x2.shape
    assert n % tm == 0, (n, tm)
    cols = w_bf16.shape[1]
    kern = functools.partial(_in_proj_kernel, q_scale=LOG2_E / math.sqrt(SB_HEAD_DIM))
    return pl.pallas_call(
        kern,
        grid=(n // tm,),
        in_specs=[
            pl.BlockSpec((tm, d), lambda i: (i, 0)),
            pl.BlockSpec((1, d), lambda i: (0, 0)),
            pl.BlockSpec((d, cols), lambda i: (0, 0)),
        ],
        out_specs=[
            pl.BlockSpec((tm, CONV_CH), lambda i: (i, 0)),
            pl.BlockSpec((tm, SB_WIDTH), lambda i: (i, 0)),
            pl.BlockSpec((tm, SB_WIDTH), lambda i: (i, 0)),
            pl.BlockSpec((tm, SB_WIDTH), lambda i: (i, 0)),
        ],
        out_shape=[
            jax.ShapeDtypeStruct((n, CONV_CH), F32),
            jax.ShapeDtypeStruct((n, SB_WIDTH), BF16),
            jax.ShapeDtypeStruct((n, SB_WIDTH), BF16),
            jax.ShapeDtypeStruct((n, SB_WIDTH), BF16),
        ],
        compiler_params=pltpu.CompilerParams(
            dimension_semantics=("arbitrary",), vmem_limit_bytes=VMEM_LIMIT_BYTES),
        name="in_proj",
    )(x2, g, w_bf16)


def _conv_kernel(glu_ref, halo_ref, cw_ref, cb_ref, lg_ref, lb_ref, o_ref, buf_ref, sh_ref, *,
                 ts, rc):
    i = pl.program_id(1)
    buf_ref[0:CONV_HALO, :] = jnp.where(i > 0, halo_ref[0], 0.0)
    buf_ref[CONV_HALO:, :] = glu_ref[0]
    cb = cb_ref[...]
    lg = lg_ref[...]
    lb = lb_ref[...]
    base = CONV_HALO - (CONV_WIDTH - 1)
    for b in range(SUBLANES):
        span = ts + SUBLANES * ((CONV_WIDTH - 1 - b) // SUBLANES)
        sh_ref[b, 0:span // SUBLANES] = buf_ref[pl.ds(base + b, span), :].reshape(
            span // SUBLANES, SUBLANES, CONV_CH)
    tr, tc = ts // SUBLANES, rc // SUBLANES
    for r in range(0, tr, tc):
        acc = jnp.broadcast_to(cb, (tc, SUBLANES, CONV_CH))
        for w in range(CONV_WIDTH):
            a, b = divmod(w, SUBLANES)
            acc = acc + sh_ref[b, r + a:r + a + tc] * cw_ref[w]
        mu = jnp.mean(acc, axis=-1, keepdims=True)
        cen = acc - mu
        var = jnp.mean(cen * cen, axis=-1, keepdims=True)
        y = cen * lax.rsqrt(var + EPS) * lg + lb
        o_ref[0, r * SUBLANES:(r + tc) * SUBLANES, :] = (
            (y * jax.nn.sigmoid(y)).reshape(rc, CONV_CH).astype(o_ref.dtype))


def _conv(glu, cw, cb, lg, lb, ts, rc):
    b, s, c = glu.shape
    assert s % ts == 0 and ts % rc == 0 and ts % CONV_HALO == 0, (s, ts, rc)
    kern = functools.partial(_conv_kernel, ts=ts, rc=rc)
    hb = ts // CONV_HALO
    return pl.pallas_call(
        kern,
        grid=(b, s // ts),
        in_specs=[
            pl.BlockSpec((1, ts, c), lambda bi, i: (bi, i, 0)),
            pl.BlockSpec((1, CONV_HALO, c), lambda bi, i: (bi, jnp.maximum(i * hb - 1, 0), 0)),
            pl.BlockSpec((CONV_WIDTH, SUBLANES, c), lambda bi, i: (0, 0, 0)),
            pl.BlockSpec((1, c), lambda bi, i: (0, 0)),
            pl.BlockSpec((1, c), lambda bi, i: (0, 0)),
            pl.BlockSpec((1, c), lambda bi, i: (0, 0)),
        ],
        out_specs=pl.BlockSpec((1, ts, c), lambda bi, i: (bi, i, 0)),
        out_shape=jax.ShapeDtypeStruct((b, s, c), BF16),
        scratch_shapes=[
            pltpu.VMEM((ts + CONV_HALO, c), F32),
            pltpu.VMEM((SUBLANES, ts // SUBLANES + (CONV_WIDTH - 1) // SUBLANES, SUBLANES, c), F32),
        ],
        compiler_params=pltpu.CompilerParams(
            dimension_semantics=("arbitrary", "arbitrary"), vmem_limit_bytes=VMEM_LIMIT_BYTES),
        name="conv",
    )(glu, glu, jnp.broadcast_to(cw[:, None, :], (CONV_WIDTH, SUBLANES, c)), cb, lg, lb)


def _attn_kernel(q_ref, k_ref, v_ref, tri_ref, g_ref, o_ref, *, t, nh):
    i = pl.program_id(2)
    qb = q_ref[0]
    lane = lax.broadcasted_iota(jnp.int32, (1, nh * SB_HEAD_DIM), 1)
    zero = jnp.zeros((), BF16)
    in_head = [(lane >= h * SB_HEAD_DIM) & (lane < (h + 1) * SB_HEAD_DIM) for h in range(nh)]
    q_heads = [jnp.where(m, qb, zero) for m in in_head]
    tri = tri_ref[...]
    rows = {"lo": slice(0, t), "hi": slice(t, 2 * t), "both": slice(0, 2 * t)}

    def key_tile(ref, j):
        j = jnp.maximum(j, 0)
        return ref[0, pl.ds(pl.multiple_of(j * t, t), t), :]

    def logits(which, j):
        kt = key_tile(k_ref, j)
        return [lax.dot_general(q_heads[h][rows[which]], kt, (((1,), (1,)), ((), ())),
                                preferred_element_type=F32) for h in range(nh)]

    def scores(zs, causal=None):
        out_z, sps, tots = [], [], []
        for z in zs:
            sp = jnp.maximum(z, 0.0) + jnp.log(1.0 + jnp.exp2(-jnp.abs(z))) * LOG2_E
            if causal is not None:
                sp = jnp.where(causal, sp, 0.0)
                z = jnp.where(causal, z, MASKED_LOGIT)
            out_z.append(z)
            sps.append(sp.astype(BF16))
            tots.append(jnp.sum(sp, axis=1, keepdims=True))
        return out_z, sps, tots

    def weighted_values(j, zs, sps, carries, acc):
        vt = key_tile(v_ref, j)
        for h in range(nh):
            cs = jnp.dot(sps[h], tri, preferred_element_type=F32)
            p = jnp.exp2(zs[h] - cs - carries[h])
            acc = acc + jnp.dot(p.astype(BF16), jnp.where(in_head[h], vt, zero),
                                preferred_element_type=F32)
        return acc

    def min_carry(carries):
        return jnp.min(functools.reduce(jnp.minimum, carries))

    def add(xs, ys):
        return tuple(x + y for x, y in zip(xs, ys))

    causal = (lax.broadcasted_iota(jnp.int32, (t, t), 1)
              < lax.broadcasted_iota(jnp.int32, (t, t), 0))
    z_shared = logits("both", 2 * i)
    z_lo0, sp_lo0, tot_lo0 = scores([z[rows["lo"]] for z in z_shared], causal)
    z_hi0, sp_hi0, tot_hi0 = scores(logits("hi", 2 * i + 1), causal)
    z_hi1, sp_hi1, tot_hi1 = scores([z[rows["hi"]] for z in z_shared])
    z_lo1, sp_lo1, tot_lo1 = scores(logits("lo", 2 * i - 1))
    c0 = (jnp.zeros((t, 1), F32),) * nh
    no_lo1 = jnp.where(i == 0, jnp.inf, 0.0).astype(F32)
    acc0 = jnp.zeros((t, nh * SB_HEAD_DIM), F32)
    acc_lo = weighted_values(2 * i, z_lo0, sp_lo0, c0, acc0)
    acc_hi = weighted_values(2 * i + 1, z_hi0, sp_hi0, c0, acc0)
    acc_hi = weighted_values(2 * i, z_hi1, sp_hi1, tot_hi0, acc_hi)
    acc_lo = weighted_values(2 * i - 1, z_lo1, sp_lo1, tuple(s + no_lo1 for s in tot_lo0), acc_lo)
    carry_lo = add(tot_lo0, tot_lo1)
    carry_hi = add(tot_hi0, tot_hi1)

    def lowest(m, carry_lo, carry_hi):
        lo_left = 2 * i - 2 - m >= 0
        return jnp.minimum(min_carry(carry_hi), jnp.where(lo_left, min_carry(carry_lo), jnp.inf))

    def more(state):
        m, _, _, _, _, low = state
        return (2 * i - 1 - m >= 0) & (low < SWEEP_STOP_LOG2)

    def step(state):
        m, acc_lo, acc_hi, carry_lo, carry_hi, _ = state
        j_lo, j_hi = 2 * i - 2 - m, 2 * i - 1 - m
        z_l, sp_l, tot_l = scores(logits("lo", j_lo))
        z_h, sp_h, tot_h = scores(logits("hi", j_hi))
        no_lo = jnp.where(j_lo < 0, jnp.inf, 0.0).astype(F32)
        acc_lo = weighted_values(j_lo, z_l, sp_l, tuple(c + no_lo for c in carry_lo), acc_lo)
        acc_hi = weighted_values(j_hi, z_h, sp_h, carry_hi, acc_hi)
        carry_lo, carry_hi = add(carry_lo, tot_l), add(carry_hi, tot_h)
        return m + 1, acc_lo, acc_hi, carry_lo, carry_hi, lowest(m + 1, carry_lo, carry_hi)

    state = (jnp.int32(0), acc_lo, acc_hi, carry_lo, carry_hi, lowest(0, carry_lo, carry_hi))
    _, acc_lo, acc_hi, _, _, _ = lax.while_loop(more, step, state)

    for which, acc in (("lo", acc_lo), ("hi", acc_hi)):
        sq = acc * acc
        inv = jnp.zeros_like(acc)
        for h in range(nh):
            ms = jnp.sum(jnp.where(in_head[h], sq, 0.0), axis=-1, keepdims=True) * (1.0 / SB_HEAD_DIM)
            inv = jnp.where(in_head[h], lax.rsqrt(ms + EPS), inv)
        o_ref[0, rows[which], :] = (acc * inv * g_ref[0]).astype(o_ref.dtype)


def _attention(q, k, v, g_groups, t, nh):
    b, s, _ = q.shape
    assert s % (2 * t) == 0 and SB_HEADS % nh == 0, (s, t, nh)
    w = nh * SB_HEAD_DIM
    tri = (lax.broadcasted_iota(jnp.int32, (t, t), 0)
           >= lax.broadcasted_iota(jnp.int32, (t, t), 1)).astype(BF16)
    kern = functools.partial(_attn_kernel, t=t, nh=nh)
    return pl.pallas_call(
        kern,
        grid=(b, SB_HEADS // nh, s // (2 * t)),
        in_specs=[
            pl.BlockSpec((1, 2 * t, w), lambda bi, p, i: (bi, i, p)),
            pl.BlockSpec((1, s, w), lambda bi, p, i: (bi, 0, p)),
            pl.BlockSpec((1, s, w), lambda bi, p, i: (bi, 0, p)),
            pl.BlockSpec((t, t), lambda bi, p, i: (0, 0)),
            pl.BlockSpec((1, 1, w), lambda bi, p, i: (p, 0, 0)),
        ],
        out_specs=pl.BlockSpec((1, 2 * t, w), lambda bi, p, i: (bi, i, p)),
        out_shape=jax.ShapeDtypeStruct((b, s, SB_WIDTH), BF16),
        compiler_params=pltpu.CompilerParams(
            dimension_semantics=("arbitrary", "arbitrary", "arbitrary"),
            vmem_limit_bytes=VMEM_LIMIT_BYTES),
        name="attn",
    )(q, k, v, tri, g_groups)


def _out_ffn_kernel(x_ref, conv_ref, attn_ref, wo_ref, gpm_ref, gpf_ref, wg_ref, wu_ref, wd_ref,
                    gpo_ref, o_ref):
    y = (jnp.dot(conv_ref[...], wo_ref[0:CONV_CH, :], preferred_element_type=F32)
         + jnp.dot(attn_ref[...], wo_ref[CONV_CH:, :], preferred_element_type=F32))
    h = x_ref[...] + _rms(y, gpm_ref[...])
    f_in = _rms(h, gpf_ref[...]).astype(BF16)
    gt = jnp.dot(f_in, wg_ref[...], preferred_element_type=F32)
    up = jnp.dot(f_in, wu_ref[...], preferred_element_type=F32)
    act = (gt * jax.nn.sigmoid(gt) * up).astype(BF16)
    f = jnp.dot(act, wd_ref[...], preferred_element_type=F32)
    o_ref[...] = h + _rms(f, gpo_ref[...])


def _out_ffn(x2, conv2, attn2, wo, gpm, gpf, wg, wu, wd, gpo, tm):
    n, d = x2.shape
    assert n % tm == 0, (n, tm)
    dff = wg.shape[1]
    const = lambda i: (0, 0)
    resident = dict(pipeline_mode=pl.Buffered(1))
    return pl.pallas_call(
        _out_ffn_kernel,
        grid=(n // tm,),
        in_specs=[
            pl.BlockSpec((tm, d), lambda i: (i, 0)),
            pl.BlockSpec((tm, CONV_CH), lambda i: (i, 0)),
            pl.BlockSpec((tm, SB_WIDTH), lambda i: (i, 0)),
            pl.BlockSpec((CONV_CH + SB_WIDTH, d), const, **resident),
            pl.BlockSpec((1, d), const),
            pl.BlockSpec((1, d), const),
            pl.BlockSpec((d, dff), const, **resident),
            pl.BlockSpec((d, dff), const, **resident),
            pl.BlockSpec((dff, d), const, **resident),
            pl.BlockSpec((1, d), const),
        ],
        out_specs=pl.BlockSpec((tm, d), lambda i: (i, 0)),
        out_shape=jax.ShapeDtypeStruct((n, d), F32),
        compiler_params=pltpu.CompilerParams(
            dimension_semantics=("arbitrary",), vmem_limit_bytes=VMEM_LIMIT_BYTES),
        name="out_ffn",
    )(x2, conv2, attn2, wo, gpm, gpf, wg, wu, wd, gpo)


def _layer(h, g_pre_mix, w_in, conv_w, conv_b, conv_ln_g, conv_ln_b, attn_norm_g, w_out,
           g_post_mix, g_pre_ffn, w_gate, w_up, w_down, g_post_ffn):
    b, s, d = h.shape
    n = b * s
    x2 = h.reshape(n, d)
    glu, q, k, v = _in_proj(x2, g_pre_mix.reshape(1, d), w_in.astype(BF16), min(IN_PROJ_ROWS, n))
    conv_out = _conv(glu.reshape(b, s, CONV_CH), conv_w.reshape(CONV_WIDTH, CONV_CH),
                     conv_b.reshape(1, CONV_CH), conv_ln_g.reshape(1, CONV_CH),
                     conv_ln_b.reshape(1, CONV_CH), ts=min(CONV_ROWS, s), rc=CONV_CHUNK)
    nh = ATTN_HEADS_PER_STEP
    attn_out = _attention(q.reshape(b, s, SB_WIDTH), k.reshape(b, s, SB_WIDTH),
                          v.reshape(b, s, SB_WIDTH),
                          attn_norm_g.reshape(SB_HEADS // nh, 1, nh * SB_HEAD_DIM),
                          t=min(ATTN_TILE, s // 2), nh=nh)
    out = _out_ffn(x2, conv_out.reshape(n, CONV_CH), attn_out.reshape(n, SB_WIDTH),
                   w_out.astype(BF16), g_post_mix.reshape(1, d), g_pre_ffn.reshape(1, d),
                   w_gate.astype(BF16), w_up.astype(BF16), w_down.astype(BF16),
                   g_post_ffn.reshape(1, d), min(OUT_FFN_ROWS, n))
    return out.reshape(b, s, d)


def kernel(x, g_pre_mix, w_in, conv_w, conv_b, conv_ln_g, conv_ln_b, attn_norm_g, w_out,
           g_post_mix, g_pre_ffn, w_gate, w_up, w_down, g_post_ffn):
    h = x
    for l in range(g_pre_mix.shape[0]):
        h = _layer(h, g_pre_mix[l], w_in[l], conv_w[l], conv_b[l], conv_ln_g[l], conv_ln_b[l],
                   attn_norm_g[l], w_out[l], g_post_mix[l], g_pre_ffn[l], w_gate[l], w_up[l],
                   w_down[l], g_post_ffn[l])
    return h
```

```python
import functools
import math

import jax
import jax.numpy as jnp
from jax import lax
from jax.experimental import pallas as pl
from jax.experimental.pallas import tpu as pltpu

EPS = 1e-6
CONV_CH = 512
CONV_WIDTH = 31
SB_HEADS = 8
SB_HEAD_DIM = 64
SB_WIDTH = SB_HEADS * SB_HEAD_DIM
ATTN_HEADS_PER_STEP = 4
ATTN_TILE = 256
IN_PROJ_ROWS = 1024
OUT_FFN_ROWS = 512
CONV_ROWS = 512
CONV_CHUNK = 32
MASKED_LOGIT = -1e30
SWEEP_STOP_LOG2 = 160.0
CONV_HALO = 32
SUBLANES = 8

VMEM_LIMIT_BYTES = 56 * 1024 * 1024

LOG2_E = 1.4426950408889634

F32 = jnp.float32
BF16 = jnp.bfloat16


def _rms(x, g):
    return x * lax.rsqrt(jnp.mean(x * x, axis=-1, keepdims=True) + EPS) * g


def _in_proj_kernel(x_ref, g_ref, w_ref, glu_ref, q_ref, k_ref, v_ref, *, q_scale):
    a = _rms(x_ref[...], g_ref[...])
    u = jnp.dot(a.astype(BF16), w_ref[...], preferred_element_type=F32)
    val = u[:, :CONV_CH]
    gate = u[:, CONV_CH:2 * CONV_CH]
    glu_ref[...] = val * jax.nn.sigmoid(gate)
    o = 2 * CONV_CH
    q_ref[...] = (u[:, o:o + SB_WIDTH] * q_scale).astype(BF16)
    k_ref[...] = u[:, o + SB_WIDTH:o + 2 * SB_WIDTH].astype(BF16)
    v_ref[...] = u[:, o + 2 * SB_WIDTH:o + 3 * SB_WIDTH].astype(BF16)


def _in_proj(x2, g, w_bf16, tm):
    n, d = x2.shape
    assert n % tm == 0, (n, tm)
    cols = w_bf16.shape[1]
    kern = functools.partial(_in_proj_kernel, q_scale=LOG2_E / math.sqrt(SB_HEAD_DIM))
    return pl.pallas_call(
        kern,
        grid=(n // tm,),
        in_specs=[
            pl.BlockSpec((tm, d), lambda i: (i, 0)),
            pl.BlockSpec((1, d), lambda i: (0, 0)),
            pl.BlockSpec((d, cols), lambda i: (0, 0)),
        ],
        out_specs=[
            pl.BlockSpec((tm, CONV_CH), lambda i: (i, 0)),
            pl.BlockSpec((tm, SB_WIDTH), lambda i: (i, 0)),
            pl.BlockSpec((tm, SB_WIDTH), lambda i: (i, 0)),
            pl.BlockSpec((tm, SB_WIDTH), lambda i: (i, 0)),
        ],
        out_shape=[
            jax.ShapeDtypeStruct((n, CONV_CH), F32),
            jax.ShapeDtypeStruct((n, SB_WIDTH), BF16),
            jax.ShapeDtypeStruct((n, SB_WIDTH), BF16),
            jax.ShapeDtypeStruct((n, SB_WIDTH), BF16),
        ],
        compiler_params=pltpu.CompilerParams(
            dimension_semantics=("arbitrary",), vmem_limit_bytes=VMEM_LIMIT_BYTES),
        name="in_proj",
    )(x2, g, w_bf16)


def _conv_kernel(glu_ref, halo_ref, cw_ref, cb_ref, lg_ref, lb_ref, o_ref, buf_ref, sh_ref, *,
                 ts, rc):
    i = pl.program_id(1)
    buf_ref[0:CONV_HALO, :] = jnp.where(i > 0, halo_ref[0], 0.0)
    buf_ref[CONV_HALO:, :] = glu_ref[0]
    cb = cb_ref[...]
    lg = lg_ref[...]
    lb = lb_ref[...]
    base = CONV_HALO - (CONV_WIDTH - 1)
    for b in range(SUBLANES):
        span = ts + SUBLANES * ((CONV_WIDTH - 1 - b) // SUBLANES)
        sh_ref[b, 0:span // SUBLANES] = buf_ref[pl.ds(base + b, span), :].reshape(
            span // SUBLANES, SUBLANES, CONV_CH)
    tr, tc = ts // SUBLANES, rc // SUBLANES
    for r in range(0, tr, tc):
        acc = jnp.broadcast_to(cb, (tc, SUBLANES, CONV_CH))
        for w in range(CONV_WIDTH):
            a, b = divmod(w, SUBLANES)
            acc = acc + sh_ref[b, r + a:r + a + tc] * cw_ref[w]
        mu = jnp.mean(acc, axis=-1, keepdims=True)
        cen = acc - mu
        var = jnp.mean(cen * cen, axis=-1, keepdims=True)
        y = cen * lax.rsqrt(var + EPS) * lg + lb
        o_ref[0, r * SUBLANES:(r + tc) * SUBLANES, :] = (
            (y * jax.nn.sigmoid(y)).reshape(rc, CONV_CH).astype(o_ref.dtype))


def _conv(glu, cw, cb, lg, lb, ts, rc):
    b, s, c = glu.shape
    assert s % ts == 0 and ts % rc == 0 and ts % CONV_HALO == 0, (s, ts, rc)
    kern = functools.partial(_conv_kernel, ts=ts, rc=rc)
    hb = ts // CONV_HALO
    return pl.pallas_call(
        kern,
        grid=(b, s // ts),
        in_specs=[
            pl.BlockSpec((1, ts, c), lambda bi, i: (bi, i, 0)),
            pl.BlockSpec((1, CONV_HALO, c), lambda bi, i: (bi, jnp.maximum(i * hb - 1, 0), 0)),
            pl.BlockSpec((CONV_WIDTH, SUBLANES, c), lambda bi, i: (0, 0, 0)),
            pl.BlockSpec((1, c), lambda bi, i: (0, 0)),
            pl.BlockSpec((1, c), lambda bi, i: (0, 0)),
            pl.BlockSpec((1, c), lambda bi, i: (0, 0)),
        ],
        out_specs=pl.BlockSpec((1, ts, c), lambda bi, i: (bi, i, 0)),
        out_shape=jax.ShapeDtypeStruct((b, s, c), BF16),
        scratch_shapes=[
            pltpu.VMEM((ts + CONV_HALO, c), F32),
            pltpu.VMEM((SUBLANES, ts // SUBLANES + (CONV_WIDTH - 1) // SUBLANES, SUBLANES, c), F32),
        ],
        compiler_params=pltpu.CompilerParams(
            dimension_semantics=("arbitrary", "arbitrary"), vmem_limit_bytes=VMEM_LIMIT_BYTES),
        name="conv",
    )(glu, glu, jnp.broadcast_to(cw[:, None, :], (CONV_WIDTH, SUBLANES, c)), cb, lg, lb)


def _attn_kernel(q_ref, k_ref, v_ref, tri_ref, g_ref, o_ref, *, t, nh):
    i = pl.program_id(2)
    qb = q_ref[0]
    lane = lax.broadcasted_iota(jnp.int32, (1, nh * SB_HEAD_DIM), 1)
    zero = jnp.zeros((), BF16)
    in_head = [(lane >= h * SB_HEAD_DIM) & (lane < (h + 1) * SB_HEAD_DIM) for h in range(nh)]
    q_heads = [jnp.where(m, qb, zero) for m in in_head]
    tri = tri_ref[...]
    rows = {"lo": slice(0, t), "hi": slice(t, 2 * t), "both": slice(0, 2 * t)}

    def key_tile(ref, j):
        j = jnp.maximum(j, 0)
        return ref[0, pl.ds(pl.multiple_of(j * t, t), t), :]

    def logits(which, j):
        kt = key_tile(k_ref, j)
        return [lax.dot_general(q_heads[h][rows[which]], kt, (((1,), (1,)), ((), ())),
                                preferred_element_type=F32) for h in range(nh)]

    def scores(zs, causal=None):
        out_z, css, tots = [], [], []
        for z in zs:
            sp = jnp.maximum(z, 0.0) + jnp.log(1.0 + jnp.exp2(-jnp.abs(z))) * LOG2_E
            if causal is not None:
                sp = jnp.where(causal, sp, 0.0)
                z = jnp.where(causal, z, MASKED_LOGIT)
            cs = jnp.dot(sp.astype(BF16), tri, preferred_element_type=F32)
            out_z.append(z)
            css.append(cs)
            tots.append(cs[:, 0:1])
        return out_z, css, tots

    def weighted_values(j, zs, css, carries, acc):
        vt = key_tile(v_ref, j)
        for h in range(nh):
            p = jnp.exp2(zs[h] - css[h] - carries[h])
            acc = acc + jnp.dot(p.astype(BF16), jnp.where(in_head[h], vt, zero),
                                preferred_element_type=F32)
        return acc

    def min_carry(carries):
        return jnp.min(functools.reduce(jnp.minimum, carries))

    def add(xs, ys):
        return tuple(x + y for x, y in zip(xs, ys))

    causal = (lax.broadcasted_iota(jnp.int32, (t, t), 1)
              < lax.broadcasted_iota(jnp.int32, (t, t), 0))
    c0 = (jnp.zeros((t, 1), F32),) * nh
    no_lo1 = jnp.where(i == 0, jnp.inf, 0.0).astype(F32)
    acc0 = jnp.zeros((t, nh * SB_HEAD_DIM), F32)
    z_shared = logits("both", 2 * i)
    z_lo0, cs_lo0, tot_lo0 = scores([z[rows["lo"]] for z in z_shared], causal)
    acc_lo = weighted_values(2 * i, z_lo0, cs_lo0, c0, acc0)
    z_hi0, cs_hi0, tot_hi0 = scores(logits("hi", 2 * i + 1), causal)
    acc_hi = weighted_values(2 * i + 1, z_hi0, cs_hi0, c0, acc0)
    z_hi1, cs_hi1, tot_hi1 = scores([z[rows["hi"]] for z in z_shared])
    acc_hi = weighted_values(2 * i, z_hi1, cs_hi1, tot_hi0, acc_hi)
    z_lo1, cs_lo1, tot_lo1 = scores(logits("lo", 2 * i - 1))
    acc_lo = weighted_values(2 * i - 1, z_lo1, cs_lo1, tuple(s + no_lo1 for s in tot_lo0), acc_lo)
    carry_lo = add(tot_lo0, tot_lo1)
    carry_hi = add(tot_hi0, tot_hi1)

    def lowest(m, carry_lo, carry_hi):
        lo_left = 2 * i - 2 - m >= 0
        return jnp.minimum(min_carry(carry_hi), jnp.where(lo_left, min_carry(carry_lo), jnp.inf))

    def more(state):
        m, _, _, _, _, low = state
        return (2 * i - 1 - m >= 0) & (low < SWEEP_STOP_LOG2)

    def step(state):
        m, acc_lo, acc_hi, carry_lo, carry_hi, _ = state
        j_lo, j_hi = 2 * i - 2 - m, 2 * i - 1 - m
        z_l, cs_l, tot_l = scores(logits("lo", j_lo))
        z_h, cs_h, tot_h = scores(logits("hi", j_hi))
        no_lo = jnp.where(j_lo < 0, jnp.inf, 0.0).astype(F32)
        acc_lo = weighted_values(j_lo, z_l, cs_l, tuple(c + no_lo for c in carry_lo), acc_lo)
        acc_hi = weighted_values(j_hi, z_h, cs_h, carry_hi, acc_hi)
        carry_lo, carry_hi = add(carry_lo, tot_l), add(carry_hi, tot_h)
        return m + 1, acc_lo, acc_hi, carry_lo, carry_hi, lowest(m + 1, carry_lo, carry_hi)

    def sweep_rest(acc_lo, acc_hi, carry_lo, carry_hi, low):
        state = (jnp.int32(0), acc_lo, acc_hi, carry_lo, carry_hi, low)
        _, acc_lo, acc_hi, _, _, _ = lax.while_loop(more, step, state)
        return acc_lo, acc_hi

    low = lowest(0, carry_lo, carry_hi)
    acc_lo, acc_hi = lax.cond((i > 0) & (low < SWEEP_STOP_LOG2), sweep_rest,
                              lambda acc_lo, acc_hi, *_: (acc_lo, acc_hi),
                              acc_lo, acc_hi, carry_lo, carry_hi, low)

    for which, acc in (("lo", acc_lo), ("hi", acc_hi)):
        sq = acc * acc
        inv = jnp.zeros_like(acc)
        for h in range(nh):
            ms = jnp.sum(jnp.where(in_head[h], sq, 0.0), axis=-1, keepdims=True) * (1.0 / SB_HEAD_DIM)
            inv = jnp.where(in_head[h], lax.rsqrt(ms + EPS), inv)
        o_ref[0, rows[which], :] = (acc * inv * g_ref[0]).astype(o_ref.dtype)


def _attention(q, k, v, g_groups, t, nh):
    b, s, _ = q.shape
    assert s % (2 * t) == 0 and SB_HEADS % nh == 0, (s, t, nh)
    w = nh * SB_HEAD_DIM
    tri = (lax.broadcasted_iota(jnp.int32, (t, t), 0)
           >= lax.broadcasted_iota(jnp.int32, (t, t), 1)).astype(BF16)
    kern = functools.partial(_attn_kernel, t=t, nh=nh)
    return pl.pallas_call(
        kern,
        grid=(b, SB_HEADS // nh, s // (2 * t)),
        in_specs=[
            pl.BlockSpec((1, 2 * t, w), lambda bi, p, i: (bi, i, p)),
            pl.BlockSpec((1, s, w), lambda bi, p, i: (bi, 0, p)),
            pl.BlockSpec((1, s, w), lambda bi, p, i: (bi, 0, p)),
            pl.BlockSpec((t, t), lambda bi, p, i: (0, 0)),
            pl.BlockSpec((1, 1, w), lambda bi, p, i: (p, 0, 0)),
        ],
        out_specs=pl.BlockSpec((1, 2 * t, w), lambda bi, p, i: (bi, i, p)),
        out_shape=jax.ShapeDtypeStruct((b, s, SB_WIDTH), BF16),
        compiler_params=pltpu.CompilerParams(
            dimension_semantics=("arbitrary", "arbitrary", "arbitrary"),
            vmem_limit_bytes=VMEM_LIMIT_BYTES),
        name="attn",
    )(q, k, v, tri, g_groups)


def _out_ffn_kernel(x_ref, conv_ref, attn_ref, wo_ref, gpm_ref, gpf_ref, wg_ref, wu_ref, wd_ref,
                    gpo_ref, o_ref):
    y = (jnp.dot(conv_ref[...], wo_ref[0:CONV_CH, :], preferred_element_type=F32)
         + jnp.dot(attn_ref[...], wo_ref[CONV_CH:, :], preferred_element_type=F32))
    h = x_ref[...] + _rms(y, gpm_ref[...])
    f_in = _rms(h, gpf_ref[...]).astype(BF16)
    gt = jnp.dot(f_in, wg_ref[...], preferred_element_type=F32)
    up = jnp.dot(f_in, wu_ref[...], preferred_element_type=F32)
    act = (gt * jax.nn.sigmoid(gt) * up).astype(BF16)
    f = jnp.dot(act, wd_ref[...], preferred_element_type=F32)
    o_ref[...] = h + _rms(f, gpo_ref[...])


def _out_ffn(x2, conv2, attn2, wo, gpm, gpf, wg, wu, wd, gpo, tm):
    n, d = x2.shape
    assert n % tm == 0, (n, tm)
    dff = wg.shape[1]
    const = lambda i: (0, 0)
    resident = dict(pipeline_mode=pl.Buffered(1))
    return pl.pallas_call(
        _out_ffn_kernel,
        grid=(n // tm,),
        in_specs=[
            pl.BlockSpec((tm, d), lambda i: (i, 0)),
            pl.BlockSpec((tm, CONV_CH), lambda i: (i, 0)),
            pl.BlockSpec((tm, SB_WIDTH), lambda i: (i, 0)),
            pl.BlockSpec((CONV_CH + SB_WIDTH, d), const, **resident),
            pl.BlockSpec((1, d), const),
            pl.BlockSpec((1, d), const),
            pl.BlockSpec((d, dff), const, **resident),
            pl.BlockSpec((d, dff), const, **resident),
            pl.BlockSpec((dff, d), const, **resident),
            pl.BlockSpec((1, d), const),
        ],
        out_specs=pl.BlockSpec((tm, d), lambda i: (i, 0)),
        out_shape=jax.ShapeDtypeStruct((n, d), F32),
        compiler_params=pltpu.CompilerParams(
            dimension_semantics=("arbitrary",), vmem_limit_bytes=VMEM_LIMIT_BYTES),
        name="out_ffn",
    )(x2, conv2, attn2, wo, gpm, gpf, wg, wu, wd, gpo)


def _layer(h, g_pre_mix, w_in, conv_w, conv_b, conv_ln_g, conv_ln_b, attn_norm_g, w_out,
           g_post_mix, g_pre_ffn, w_gate, w_up, w_down, g_post_ffn):
    b, s, d = h.shape
    n = b * s
    x2 = h.reshape(n, d)
    glu, q, k, v = _in_proj(x2, g_pre_mix.reshape(1, d), w_in.astype(BF16), min(IN_PROJ_ROWS, n))
    conv_out = _conv(glu.reshape(b, s, CONV_CH), conv_w.reshape(CONV_WIDTH, CONV_CH),
                     conv_b.reshape(1, CONV_CH), conv_ln_g.reshape(1, CONV_CH),
                     conv_ln_b.reshape(1, CONV_CH), ts=min(CONV_ROWS, s), rc=CONV_CHUNK)
    nh = ATTN_HEADS_PER_STEP
    attn_out = _attention(q.reshape(b, s, SB_WIDTH), k.reshape(b, s, SB_WIDTH),
                          v.reshape(b, s, SB_WIDTH),
                          attn_norm_g.reshape(SB_HEADS // nh, 1, nh * SB_HEAD_DIM),
                          t=min(ATTN_TILE, s // 2), nh=nh)
    out = _out_ffn(x2, conv_out.reshape(n, CONV_CH), attn_out.reshape(n, SB_WIDTH),
                   w_out.astype(BF16), g_post_mix.reshape(1, d), g_pre_ffn.reshape(1, d),
                   w_gate.astype(BF16), w_up.astype(BF16), w_down.astype(BF16),
                   g_post_ffn.reshape(1, d), min(OUT_FFN_ROWS, n))
    return out.reshape(b, s, d)


def kernel(x, g_pre_mix, w_in, conv_w, conv_b, conv_ln_g, conv_ln_b, attn_norm_g, w_out,
           g_post_mix, g_pre_ffn, w_gate, w_up, w_down, g_post_ffn):
    h = x
    for l in range(g_pre_mix.shape[0]):
        h = _layer(h, g_pre_mix[l], w_in[l], conv_w[l], conv_b[l], conv_ln_g[l], conv_ln_b[l],
                   attn_norm_g[l], w_out[l], g_post_mix[l], g_pre_ffn[l], w_gate[l], w_up[l],
                   w_down[l], g_post_ffn[l])
    return h
```

```python
import functools
import math

import jax
import jax.numpy as jnp
from jax import lax
from jax.experimental import pallas as pl
from jax.experimental.pallas import tpu as pltpu

EPS = 1e-6
CONV_CH = 512
CONV_WIDTH = 31
SB_HEADS = 8
SB_HEAD_DIM = 64
SB_WIDTH = SB_HEADS * SB_HEAD_DIM
ATTN_HEADS_PER_STEP = 4
ATTN_TILE = 256
IN_PROJ_ROWS = 1024
OUT_FFN_ROWS = 1024
CONV_ROWS = 1024
CONV_CHUNK = 32
MASKED_LOGIT = -1e30
SWEEP_STOP_LOG2 = 160.0
CONV_HALO = 32
SUBLANES = 8

VMEM_LIMIT_BYTES = 56 * 1024 * 1024

LOG2_E = 1.4426950408889634

F32 = jnp.float32
BF16 = jnp.bfloat16


def _rms(x, g):
    return x * lax.rsqrt(jnp.mean(x * x, axis=-1, keepdims=True) + EPS) * g


def _in_proj_kernel(x_ref, g_ref, w_ref, glu_ref, q_ref, k_ref, v_ref, *, q_scale):
    a = _rms(x_ref[...], g_ref[...])
    u = jnp.dot(a.astype(BF16), w_ref[...], preferred_element_type=F32)
    val = u[:, :CONV_CH]
    gate = u[:, CONV_CH:2 * CONV_CH]
    glu_ref[...] = val * jax.nn.sigmoid(gate)
    o = 2 * CONV_CH
    q_ref[...] = (u[:, o:o + SB_WIDTH] * q_scale).astype(BF16)
    k_ref[...] = u[:, o + SB_WIDTH:o + 2 * SB_WIDTH].astype(BF16)
    v_ref[...] = u[:, o + 2 * SB_WIDTH:o + 3 * SB_WIDTH].astype(BF16)


def _in_proj(x2, g, w_bf16, tm):
    n, d = x2.shape
    assert n % tm == 0, (n, tm)
    cols = w_bf16.shape[1]
    kern = functools.partial(_in_proj_kernel, q_scale=LOG2_E / math.sqrt(SB_HEAD_DIM))
    return pl.pallas_call(
        kern,
        grid=(n // tm,),
        in_specs=[
            pl.BlockSpec((tm, d), lambda i: (i, 0)),
            pl.BlockSpec((1, d), lambda i: (0, 0)),
            pl.BlockSpec((d, cols), lambda i: (0, 0)),
        ],
        out_specs=[
            pl.BlockSpec((tm, CONV_CH), lambda i: (i, 0)),
            pl.BlockSpec((tm, SB_WIDTH), lambda i: (i, 0)),
            pl.BlockSpec((tm, SB_WIDTH), lambda i: (i, 0)),
            pl.BlockSpec((tm, SB_WIDTH), lambda i: (i, 0)),
        ],
        out_shape=[
            jax.ShapeDtypeStruct((n, CONV_CH), F32),
            jax.ShapeDtypeStruct((n, SB_WIDTH), BF16),
            jax.ShapeDtypeStruct((n, SB_WIDTH), BF16),
            jax.ShapeDtypeStruct((n, SB_WIDTH), BF16),
        ],
        compiler_params=pltpu.CompilerParams(
            dimension_semantics=("arbitrary",), vmem_limit_bytes=VMEM_LIMIT_BYTES),
        name="in_proj",
    )(x2, g, w_bf16)


def _conv_kernel(glu_ref, halo_ref, cw_ref, cb_ref, lg_ref, lb_ref, o_ref, buf_ref, sh_ref, *,
                 ts, rc):
    i = pl.program_id(1)
    buf_ref[0:CONV_HALO, :] = jnp.where(i > 0, halo_ref[0], 0.0)
    buf_ref[CONV_HALO:, :] = glu_ref[0]
    cb = cb_ref[...]
    lg = lg_ref[...]
    lb = lb_ref[...]
    base = CONV_HALO - (CONV_WIDTH - 1)
    for b in range(SUBLANES):
        span = ts + SUBLANES * ((CONV_WIDTH - 1 - b) // SUBLANES)
        sh_ref[b, 0:span // SUBLANES] = buf_ref[pl.ds(base + b, span), :].reshape(
            span // SUBLANES, SUBLANES, CONV_CH)
    tr, tc = ts // SUBLANES, rc // SUBLANES
    for r in range(0, tr, tc):
        acc = jnp.broadcast_to(cb, (tc, SUBLANES, CONV_CH))
        for w in range(CONV_WIDTH):
            a, b = divmod(w, SUBLANES)
            acc = acc + sh_ref[b, r + a:r + a + tc] * cw_ref[w]
        mu = jnp.mean(acc, axis=-1, keepdims=True)
        cen = acc - mu
        var = jnp.mean(cen * cen, axis=-1, keepdims=True)
        y = cen * lax.rsqrt(var + EPS) * lg + lb
        o_ref[0, r * SUBLANES:(r + tc) * SUBLANES, :] = (
            (y * jax.nn.sigmoid(y)).reshape(rc, CONV_CH).astype(o_ref.dtype))


def _conv(glu, cw, cb, lg, lb, ts, rc):
    b, s, c = glu.shape
    assert s % ts == 0 and ts % rc == 0 and ts % CONV_HALO == 0, (s, ts, rc)
    kern = functools.partial(_conv_kernel, ts=ts, rc=rc)
    hb = ts // CONV_HALO
    return pl.pallas_call(
        kern,
        grid=(b, s // ts),
        in_specs=[
            pl.BlockSpec((1, ts, c), lambda bi, i: (bi, i, 0)),
            pl.BlockSpec((1, CONV_HALO, c), lambda bi, i: (bi, jnp.maximum(i * hb - 1, 0), 0)),
            pl.BlockSpec((CONV_WIDTH, SUBLANES, c), lambda bi, i: (0, 0, 0)),
            pl.BlockSpec((1, c), lambda bi, i: (0, 0)),
            pl.BlockSpec((1, c), lambda bi, i: (0, 0)),
            pl.BlockSpec((1, c), lambda bi, i: (0, 0)),
        ],
        out_specs=pl.BlockSpec((1, ts, c), lambda bi, i: (bi, i, 0)),
        out_shape=jax.ShapeDtypeStruct((b, s, c), BF16),
        scratch_shapes=[
            pltpu.VMEM((ts + CONV_HALO, c), F32),
            pltpu.VMEM((SUBLANES, ts // SUBLANES + (CONV_WIDTH - 1) // SUBLANES, SUBLANES, c), F32),
        ],
        compiler_params=pltpu.CompilerParams(
            dimension_semantics=("arbitrary", "arbitrary"), vmem_limit_bytes=VMEM_LIMIT_BYTES),
        name="conv",
    )(glu, glu, jnp.broadcast_to(cw[:, None, :], (CONV_WIDTH, SUBLANES, c)), cb, lg, lb)


def _attn_kernel(q_ref, k_ref, v_ref, tri_ref, g_ref, o_ref, *, t, nh):
    i = pl.program_id(2)
    qb = q_ref[0]
    lane = lax.broadcasted_iota(jnp.int32, (1, nh * SB_HEAD_DIM), 1)
    zero = jnp.zeros((), BF16)
    in_head = [(lane >= h * SB_HEAD_DIM) & (lane < (h + 1) * SB_HEAD_DIM) for h in range(nh)]
    q_heads = [jnp.where(m, qb, zero) for m in in_head]
    tri = tri_ref[...]
    rows = {"lo": slice(0, t), "hi": slice(t, 2 * t), "both": slice(0, 2 * t)}

    def key_tile(ref, j):
        j = jnp.maximum(j, 0)
        return ref[0, pl.ds(pl.multiple_of(j * t, t), t), :]

    def logits(which, j):
        kt = key_tile(k_ref, j)
        return [lax.dot_general(q_heads[h][rows[which]], kt, (((1,), (1,)), ((), ())),
                                preferred_element_type=F32) for h in range(nh)]

    def scores(zs, causal=None):
        out_z, css, tots = [], [], []
        for z in zs:
            sp = jnp.maximum(z, 0.0) + jnp.log(1.0 + jnp.exp2(-jnp.abs(z))) * LOG2_E
            if causal is not None:
                sp = jnp.where(causal, sp, 0.0)
                z = jnp.where(causal, z, MASKED_LOGIT)
            cs = jnp.dot(sp.astype(BF16), tri, preferred_element_type=F32)
            out_z.append(z)
            css.append(cs)
            tots.append(cs[:, 0:1])
        return out_z, css, tots

    def weighted_values(j, zs, css, carries, acc):
        vt = key_tile(v_ref, j)
        for h in range(nh):
            p = jnp.exp2(zs[h] - css[h] - carries[h])
            acc = acc + jnp.dot(p.astype(BF16), jnp.where(in_head[h], vt, zero),
                                preferred_element_type=F32)
        return acc

    def min_carry(carries):
        return jnp.min(functools.reduce(jnp.minimum, carries))

    def add(xs, ys):
        return tuple(x + y for x, y in zip(xs, ys))

    causal = (lax.broadcasted_iota(jnp.int32, (t, t), 1)
              < lax.broadcasted_iota(jnp.int32, (t, t), 0))
    c0 = (jnp.zeros((t, 1), F32),) * nh
    no_lo1 = jnp.where(i == 0, jnp.inf, 0.0).astype(F32)
    acc0 = jnp.zeros((t, nh * SB_HEAD_DIM), F32)
    z_shared = logits("both", 2 * i)
    z_lo0, cs_lo0, tot_lo0 = scores([z[rows["lo"]] for z in z_shared], causal)
    acc_lo = weighted_values(2 * i, z_lo0, cs_lo0, c0, acc0)
    z_hi0, cs_hi0, tot_hi0 = scores(logits("hi", 2 * i + 1), causal)
    acc_hi = weighted_values(2 * i + 1, z_hi0, cs_hi0, c0, acc0)
    z_hi1, cs_hi1, tot_hi1 = scores([z[rows["hi"]] for z in z_shared])
    acc_hi = weighted_values(2 * i, z_hi1, cs_hi1, tot_hi0, acc_hi)
    z_lo1, cs_lo1, tot_lo1 = scores(logits("lo", 2 * i - 1))
    acc_lo = weighted_values(2 * i - 1, z_lo1, cs_lo1, tuple(s + no_lo1 for s in tot_lo0), acc_lo)
    carry_lo = add(tot_lo0, tot_lo1)
    carry_hi = add(tot_hi0, tot_hi1)

    def lowest(m, carry_lo, carry_hi):
        lo_left = 2 * i - 2 - m >= 0
        return jnp.minimum(min_carry(carry_hi), jnp.where(lo_left, min_carry(carry_lo), jnp.inf))

    def more(state):
        m, _, _, _, _, low = state
        return (2 * i - 1 - m >= 0) & (low < SWEEP_STOP_LOG2)

    def step(state):
        m, acc_lo, acc_hi, carry_lo, carry_hi, _ = state
        j_lo, j_hi = 2 * i - 2 - m, 2 * i - 1 - m
        z_l, cs_l, tot_l = scores(logits("lo", j_lo))
        z_h, cs_h, tot_h = scores(logits("hi", j_hi))
        no_lo = jnp.where(j_lo < 0, jnp.inf, 0.0).astype(F32)
        acc_lo = weighted_values(j_lo, z_l, cs_l, tuple(c + no_lo for c in carry_lo), acc_lo)
        acc_hi = weighted_values(j_hi, z_h, cs_h, carry_hi, acc_hi)
        carry_lo, carry_hi = add(carry_lo, tot_l), add(carry_hi, tot_h)
        return m + 1, acc_lo, acc_hi, carry_lo, carry_hi, lowest(m + 1, carry_lo, carry_hi)

    def sweep_rest(acc_lo, acc_hi, carry_lo, carry_hi, low):
        state = (jnp.int32(0), acc_lo, acc_hi, carry_lo, carry_hi, low)
        _, acc_lo, acc_hi, _, _, _ = lax.while_loop(more, step, state)
        return acc_lo, acc_hi

    low = lowest(0, carry_lo, carry_hi)
    acc_lo, acc_hi = lax.cond((i > 0) & (low < SWEEP_STOP_LOG2), sweep_rest,
                              lambda acc_lo, acc_hi, *_: (acc_lo, acc_hi),
                              acc_lo, acc_hi, carry_lo, carry_hi, low)

    for which, acc in (("lo", acc_lo), ("hi", acc_hi)):
        sq = acc * acc
        inv = jnp.zeros_like(acc)
        for h in range(nh):
            ms = jnp.sum(jnp.where(in_head[h], sq, 0.0), axis=-1, keepdims=True) * (1.0 / SB_HEAD_DIM)
            inv = jnp.where(in_head[h], lax.rsqrt(ms + EPS), inv)
        o_ref[0, rows[which], :] = (acc * inv * g_ref[0]).astype(o_ref.dtype)


def _attention(q, k, v, g_groups, t, nh):
    b, s, _ = q.shape
    assert s % (2 * t) == 0 and SB_HEADS % nh == 0, (s, t, nh)
    w = nh * SB_HEAD_DIM
    tri = (lax.broadcasted_iota(jnp.int32, (t, t), 0)
           >= lax.broadcasted_iota(jnp.int32, (t, t), 1)).astype(BF16)
    kern = functools.partial(_attn_kernel, t=t, nh=nh)
    return pl.pallas_call(
        kern,
        grid=(b, SB_HEADS // nh, s // (2 * t)),
        in_specs=[
            pl.BlockSpec((1, 2 * t, w), lambda bi, p, i: (bi, i, p)),
            pl.BlockSpec((1, s, w), lambda bi, p, i: (bi, 0, p)),
            pl.BlockSpec((1, s, w), lambda bi, p, i: (bi, 0, p)),
            pl.BlockSpec((t, t), lambda bi, p, i: (0, 0)),
            pl.BlockSpec((1, 1, w), lambda bi, p, i: (p, 0, 0)),
        ],
        out_specs=pl.BlockSpec((1, 2 * t, w), lambda bi, p, i: (bi, i, p)),
        out_shape=jax.ShapeDtypeStruct((b, s, SB_WIDTH), BF16),
        compiler_params=pltpu.CompilerParams(
            dimension_semantics=("arbitrary", "arbitrary", "arbitrary"),
            vmem_limit_bytes=VMEM_LIMIT_BYTES),
        name="attn",
    )(q, k, v, tri, g_groups)


def _out_ffn_kernel(x_ref, conv_ref, attn_ref, wo_ref, gpm_ref, gpf_ref, wg_ref, wu_ref, wd_ref,
                    gpo_ref, o_ref):
    tm = x_ref.shape[0]
    half = tm // 2
    r0, r1 = slice(0, half), slice(half, tm)

    def head(r):
        y = (jnp.dot(conv_ref[r, :], wo_ref[0:CONV_CH, :], preferred_element_type=F32)
             + jnp.dot(attn_ref[r, :], wo_ref[CONV_CH:, :], preferred_element_type=F32))
        h = x_ref[r, :] + _rms(y, gpm_ref[...])
        return h, _rms(h, gpf_ref[...]).astype(BF16)

    def gate_up(f_in):
        gt = jnp.dot(f_in, wg_ref[...], preferred_element_type=F32)
        up = jnp.dot(f_in, wu_ref[...], preferred_element_type=F32)
        return (gt * jax.nn.sigmoid(gt) * up).astype(BF16)

    def down(r, h, act):
        f = jnp.dot(act, wd_ref[...], preferred_element_type=F32)
        o_ref[r, :] = h + _rms(f, gpo_ref[...])

    h0, f0 = head(r0)
    act0 = gate_up(f0)
    h1, f1 = head(r1)
    down(r0, h0, act0)
    act1 = gate_up(f1)
    down(r1, h1, act1)


def _out_ffn(x2, conv2, attn2, wo, gpm, gpf, wg, wu, wd, gpo, tm):
    n, d = x2.shape
    assert n % tm == 0, (n, tm)
    dff = wg.shape[1]
    const = lambda i: (0, 0)
    resident = dict(pipeline_mode=pl.Buffered(1))
    return pl.pallas_call(
        _out_ffn_kernel,
        grid=(n // tm,),
        in_specs=[
            pl.BlockSpec((tm, d), lambda i: (i, 0)),
            pl.BlockSpec((tm, CONV_CH), lambda i: (i, 0)),
            pl.BlockSpec((tm, SB_WIDTH), lambda i: (i, 0)),
            pl.BlockSpec((CONV_CH + SB_WIDTH, d), const, **resident),
            pl.BlockSpec((1, d), const),
            pl.BlockSpec((1, d), const),
            pl.BlockSpec((d, dff), const, **resident),
            pl.BlockSpec((d, dff), const, **resident),
            pl.BlockSpec((dff, d), const, **resident),
            pl.BlockSpec((1, d), const),
        ],
        out_specs=pl.BlockSpec((tm, d), lambda i: (i, 0)),
        out_shape=jax.ShapeDtypeStruct((n, d), F32),
        compiler_params=pltpu.CompilerParams(
            dimension_semantics=("arbitrary",), vmem_limit_bytes=VMEM_LIMIT_BYTES),
        name="out_ffn",
    )(x2, conv2, attn2, wo, gpm, gpf, wg, wu, wd, gpo)


def _layer(h, g_pre_mix, w_in, conv_w, conv_b, conv_ln_g, conv_ln_b, attn_norm_g, w_out,
           g_post_mix, g_pre_ffn, w_gate, w_up, w_down, g_post_ffn):
    b, s, d = h.shape
    n = b * s
    x2 = h.reshape(n, d)
    glu, q, k, v = _in_proj(x2, g_pre_mix.reshape(1, d), w_in.astype(BF16), min(IN_PROJ_ROWS, n))
    conv_out = _conv(glu.reshape(b, s, CONV_CH), conv_w.reshape(CONV_WIDTH, CONV_CH),
                     conv_b.reshape(1, CONV_CH), conv_ln_g.reshape(1, CONV_CH),
                     conv_ln_b.reshape(1, CONV_CH), ts=min(CONV_ROWS, s), rc=CONV_CHUNK)
    nh = ATTN_HEADS_PER_STEP
    attn_out = _attention(q.reshape(b, s, SB_WIDTH), k.reshape(b, s, SB_WIDTH),
                          v.reshape(b, s, SB_WIDTH),
                          attn_norm_g.reshape(SB_HEADS // nh, 1, nh * SB_HEAD_DIM),
                          t=min(ATTN_TILE, s // 2), nh=nh)
    out = _out_ffn(x2, conv_out.reshape(n, CONV_CH), attn_out.reshape(n, SB_WIDTH),
                   w_out.astype(BF16), g_post_mix.reshape(1, d), g_pre_ffn.reshape(1, d),
                   w_gate.astype(BF16), w_up.astype(BF16), w_down.astype(BF16),
                   g_post_ffn.reshape(1, d), min(OUT_FFN_ROWS, n))
    return out.reshape(b, s, d)


def kernel(x, g_pre_mix, w_in, conv_w, conv_b, conv_ln_g, conv_ln_b, attn_norm_g, w_out,
           g_post_mix, g_pre_ffn, w_gate, w_up, w_down, g_post_ffn):
    h = x
    for l in range(g_pre_mix.shape[0]):
        h = _layer(h, g_pre_mix[l], w_in[l], conv_w[l], conv_b[l], conv_ln_g[l], conv_ln_b[l],
                   attn_norm_g[l], w_out[l], g_post_mix[l], g_pre_ffn[l], w_gate[l], w_up[l],
                   w_down[l], g_post_ffn[l])
    return h
```

```python
import functools
import math

import jax
import jax.numpy as jnp
from jax import lax
from jax.experimental import pallas as pl
from jax.experimental.pallas import tpu as pltpu

EPS = 1e-6
CONV_CH = 512
CONV_WIDTH = 31
SB_HEADS = 8
SB_HEAD_DIM = 64
SB_WIDTH = SB_HEADS * SB_HEAD_DIM
ATTN_HEADS_PER_STEP = 4
ATTN_TILE = 256
IN_PROJ_ROWS = 1024
OUT_FFN_ROWS = 1024
CONV_ROWS = 1024
CONV_CHUNK = 32
MASKED_LOGIT = -1e30
SWEEP_STOP_LOG2 = 160.0
CONV_HALO = 32
SUBLANES = 8

VMEM_LIMIT_BYTES = 56 * 1024 * 1024

LOG2_E = 1.4426950408889634

F32 = jnp.float32
BF16 = jnp.bfloat16


def _rms(x, g):
    return x * lax.rsqrt(jnp.mean(x * x, axis=-1, keepdims=True) + EPS) * g


def _in_proj_kernel(x_ref, g_ref, w_ref, glu_ref, q_ref, k_ref, v_ref, *, q_scale):
    a = _rms(x_ref[...], g_ref[...])
    u = jnp.dot(a.astype(BF16), w_ref[...], preferred_element_type=F32)
    val = u[:, :CONV_CH]
    gate = u[:, CONV_CH:2 * CONV_CH]
    glu_ref[...] = val * jax.nn.sigmoid(gate)
    o = 2 * CONV_CH
    q_ref[...] = (u[:, o:o + SB_WIDTH] * q_scale).astype(BF16)
    k_ref[...] = u[:, o + SB_WIDTH:o + 2 * SB_WIDTH].astype(BF16)
    v_ref[...] = u[:, o + 2 * SB_WIDTH:o + 3 * SB_WIDTH].astype(BF16)


def _in_proj(x2, g, w_bf16, tm):
    n, d = x2.shape
    assert n % tm == 0, (n, tm)
    cols = w_bf16.shape[1]
    kern = functools.partial(_in_proj_kernel, q_scale=LOG2_E / math.sqrt(SB_HEAD_DIM))
    return pl.pallas_call(
        kern,
        grid=(n // tm,),
        in_specs=[
            pl.BlockSpec((tm, d), lambda i: (i, 0)),
            pl.BlockSpec((1, d), lambda i: (0, 0)),
            pl.BlockSpec((d, cols), lambda i: (0, 0)),
        ],
        out_specs=[
            pl.BlockSpec((tm, CONV_CH), lambda i: (i, 0)),
            pl.BlockSpec((tm, SB_WIDTH), lambda i: (i, 0)),
            pl.BlockSpec((tm, SB_WIDTH), lambda i: (i, 0)),
            pl.BlockSpec((tm, SB_WIDTH), lambda i: (i, 0)),
        ],
        out_shape=[
            jax.ShapeDtypeStruct((n, CONV_CH), F32),
            jax.ShapeDtypeStruct((n, SB_WIDTH), BF16),
            jax.ShapeDtypeStruct((n, SB_WIDTH), BF16),
            jax.ShapeDtypeStruct((n, SB_WIDTH), BF16),
        ],
        compiler_params=pltpu.CompilerParams(
            dimension_semantics=("arbitrary",), vmem_limit_bytes=VMEM_LIMIT_BYTES),
        name="in_proj",
    )(x2, g, w_bf16)


def _conv_kernel(glu_ref, halo_ref, cw_ref, cb_ref, lg_ref, lb_ref, o_ref, buf_ref, sh_ref, *,
                 ts, rc):
    i = pl.program_id(1)
    buf_ref[0:CONV_HALO, :] = jnp.where(i > 0, halo_ref[0], 0.0)
    buf_ref[CONV_HALO:, :] = glu_ref[0]
    cb = cb_ref[...]
    lg = lg_ref[...]
    lb = lb_ref[...]
    base = CONV_HALO - (CONV_WIDTH - 1)
    for b in range(SUBLANES):
        span = ts + SUBLANES * ((CONV_WIDTH - 1 - b) // SUBLANES)
        sh_ref[b, 0:span // SUBLANES] = buf_ref[pl.ds(base + b, span), :].reshape(
            span // SUBLANES, SUBLANES, CONV_CH)
    tr, tc = ts // SUBLANES, rc // SUBLANES
    for r in range(0, tr, tc):
        acc = jnp.broadcast_to(cb, (tc, SUBLANES, CONV_CH))
        for w in range(CONV_WIDTH):
            a, b = divmod(w, SUBLANES)
            acc = acc + sh_ref[b, r + a:r + a + tc] * cw_ref[w]
        mu = jnp.mean(acc, axis=-1, keepdims=True)
        cen = acc - mu
        var = jnp.mean(cen * cen, axis=-1, keepdims=True)
        y = cen * lax.rsqrt(var + EPS) * lg + lb
        o_ref[0, r * SUBLANES:(r + tc) * SUBLANES, :] = (
            (y * jax.nn.sigmoid(y)).reshape(rc, CONV_CH).astype(o_ref.dtype))


def _conv(glu, cw, cb, lg, lb, ts, rc):
    b, s, c = glu.shape
    assert s % ts == 0 and ts % rc == 0 and ts % CONV_HALO == 0, (s, ts, rc)
    kern = functools.partial(_conv_kernel, ts=ts, rc=rc)
    hb = ts // CONV_HALO
    return pl.pallas_call(
        kern,
        grid=(b, s // ts),
        in_specs=[
            pl.BlockSpec((1, ts, c), lambda bi, i: (bi, i, 0)),
            pl.BlockSpec((1, CONV_HALO, c), lambda bi, i: (bi, jnp.maximum(i * hb - 1, 0), 0)),
            pl.BlockSpec((CONV_WIDTH, SUBLANES, c), lambda bi, i: (0, 0, 0)),
            pl.BlockSpec((1, c), lambda bi, i: (0, 0)),
            pl.BlockSpec((1, c), lambda bi, i: (0, 0)),
            pl.BlockSpec((1, c), lambda bi, i: (0, 0)),
        ],
        out_specs=pl.BlockSpec((1, ts, c), lambda bi, i: (bi, i, 0)),
        out_shape=jax.ShapeDtypeStruct((b, s, c), BF16),
        scratch_shapes=[
            pltpu.VMEM((ts + CONV_HALO, c), F32),
            pltpu.VMEM((SUBLANES, ts // SUBLANES + (CONV_WIDTH - 1) // SUBLANES, SUBLANES, c), F32),
        ],
        compiler_params=pltpu.CompilerParams(
            dimension_semantics=("arbitrary", "arbitrary"), vmem_limit_bytes=VMEM_LIMIT_BYTES),
        name="conv",
    )(glu, glu, jnp.broadcast_to(cw[:, None, :], (CONV_WIDTH, SUBLANES, c)), cb, lg, lb)


def _attn_kernel(q_ref, k_ref, v_ref, tri_ref, g_ref, o_ref, *, t, nh):
    i = pl.program_id(2)
    qb = q_ref[0]
    lane = lax.broadcasted_iota(jnp.int32, (1, nh * SB_HEAD_DIM), 1)
    zero = jnp.zeros((), BF16)
    in_head = [(lane >= h * SB_HEAD_DIM) & (lane < (h + 1) * SB_HEAD_DIM) for h in range(nh)]
    q_heads = [jnp.where(m, qb, zero) for m in in_head]
    tri = tri_ref[...]
    rows = {"lo": slice(0, t), "hi": slice(t, 2 * t), "both": slice(0, 2 * t)}

    def key_tile(ref, j):
        j = jnp.maximum(j, 0)
        return ref[0, pl.ds(pl.multiple_of(j * t, t), t), :]

    def logits(which, j):
        kt = key_tile(k_ref, j)
        return [lax.dot_general(q_heads[h][rows[which]], kt, (((1,), (1,)), ((), ())),
                                preferred_element_type=F32) for h in range(nh)]

    def scores(zs, causal=None):
        out_z, css, tots = [], [], []
        for z in zs:
            sp = jnp.maximum(z, 0.0) + jnp.log(1.0 + jnp.exp2(-jnp.abs(z))) * LOG2_E
            if causal is not None:
                sp = jnp.where(causal, sp, 0.0)
                z = jnp.where(causal, z, MASKED_LOGIT)
            cs = jnp.dot(sp.astype(BF16), tri, preferred_element_type=F32)
            out_z.append(z)
            css.append(cs)
            tots.append(cs[:, 0:1])
        return out_z, css, tots

    def weighted_values(j, zs, css, carries, acc):
        vt = key_tile(v_ref, j)
        for h in range(nh):
            p = jnp.exp2(zs[h] - css[h] - carries[h])
            acc = acc + jnp.dot(p.astype(BF16), jnp.where(in_head[h], vt, zero),
                                preferred_element_type=F32)
        return acc

    def min_carry(carries):
        return jnp.min(functools.reduce(jnp.minimum, carries))

    def add(xs, ys):
        return tuple(x + y for x, y in zip(xs, ys))

    causal = (lax.broadcasted_iota(jnp.int32, (t, t), 1)
              < lax.broadcasted_iota(jnp.int32, (t, t), 0))
    c0 = (jnp.zeros((t, 1), F32),) * nh
    no_lo1 = jnp.where(i == 0, jnp.inf, 0.0).astype(F32)
    acc0 = jnp.zeros((t, nh * SB_HEAD_DIM), F32)
    z_shared = logits("both", 2 * i)
    z_lo0, cs_lo0, tot_lo0 = scores([z[rows["lo"]] for z in z_shared], causal)
    acc_lo = weighted_values(2 * i, z_lo0, cs_lo0, c0, acc0)
    z_hi0, cs_hi0, tot_hi0 = scores(logits("hi", 2 * i + 1), causal)
    acc_hi = weighted_values(2 * i + 1, z_hi0, cs_hi0, c0, acc0)
    z_hi1, cs_hi1, tot_hi1 = scores([z[rows["hi"]] for z in z_shared])
    acc_hi = weighted_values(2 * i, z_hi1, cs_hi1, tot_hi0, acc_hi)
    z_lo1, cs_lo1, tot_lo1 = scores(logits("lo", 2 * i - 1))
    acc_lo = weighted_values(2 * i - 1, z_lo1, cs_lo1, tuple(s + no_lo1 for s in tot_lo0), acc_lo)
    carry_lo = add(tot_lo0, tot_lo1)
    carry_hi = add(tot_hi0, tot_hi1)

    def lowest(m, carry_lo, carry_hi):
        lo_left = 2 * i - 2 - m >= 0
        return jnp.minimum(min_carry(carry_hi), jnp.where(lo_left, min_carry(carry_lo), jnp.inf))

    def more(state):
        m, _, _, _, _, low = state
        return (2 * i - 1 - m >= 0) & (low < SWEEP_STOP_LOG2)

    def step(state):
        m, acc_lo, acc_hi, carry_lo, carry_hi, _ = state
        j_lo, j_hi = 2 * i - 2 - m, 2 * i - 1 - m
        z_l, cs_l, tot_l = scores(logits("lo", j_lo))
        z_h, cs_h, tot_h = scores(logits("hi", j_hi))
        no_lo = jnp.where(j_lo < 0, jnp.inf, 0.0).astype(F32)
        acc_lo = weighted_values(j_lo, z_l, cs_l, tuple(c + no_lo for c in carry_lo), acc_lo)
        acc_hi = weighted_values(j_hi, z_h, cs_h, carry_hi, acc_hi)
        carry_lo, carry_hi = add(carry_lo, tot_l), add(carry_hi, tot_h)
        return m + 1, acc_lo, acc_hi, carry_lo, carry_hi, lowest(m + 1, carry_lo, carry_hi)

    def write_normalized(acc_lo, acc_hi):
        for which, acc in (("lo", acc_lo), ("hi", acc_hi)):
            sq = acc * acc
            inv = jnp.zeros_like(acc)
            for h in range(nh):
                ms = jnp.sum(jnp.where(in_head[h], sq, 0.0), axis=-1, keepdims=True) * (1.0 / SB_HEAD_DIM)
                inv = jnp.where(in_head[h], lax.rsqrt(ms + EPS), inv)
            o_ref[0, rows[which], :] = (acc * inv * g_ref[0]).astype(o_ref.dtype)

    write_normalized(acc_lo, acc_hi)
    low = lowest(0, carry_lo, carry_hi)

    @pl.when((i > 0) & (low < SWEEP_STOP_LOG2))
    def _():
        state = (jnp.int32(0), acc_lo, acc_hi, carry_lo, carry_hi, low)
        _, acc_lo_end, acc_hi_end, _, _, _ = lax.while_loop(more, step, state)
        write_normalized(acc_lo_end, acc_hi_end)


def _attention(q, k, v, g_groups, t, nh):
    b, s, _ = q.shape
    assert s % (2 * t) == 0 and SB_HEADS % nh == 0, (s, t, nh)
    w = nh * SB_HEAD_DIM
    tri = (lax.broadcasted_iota(jnp.int32, (t, t), 0)
           >= lax.broadcasted_iota(jnp.int32, (t, t), 1)).astype(BF16)
    kern = functools.partial(_attn_kernel, t=t, nh=nh)
    return pl.pallas_call(
        kern,
        grid=(b, SB_HEADS // nh, s // (2 * t)),
        in_specs=[
            pl.BlockSpec((1, 2 * t, w), lambda bi, p, i: (bi, i, p)),
            pl.BlockSpec((1, s, w), lambda bi, p, i: (bi, 0, p)),
            pl.BlockSpec((1, s, w), lambda bi, p, i: (bi, 0, p)),
            pl.BlockSpec((t, t), lambda bi, p, i: (0, 0)),
            pl.BlockSpec((1, 1, w), lambda bi, p, i: (p, 0, 0)),
        ],
        out_specs=pl.BlockSpec((1, 2 * t, w), lambda bi, p, i: (bi, i, p)),
        out_shape=jax.ShapeDtypeStruct((b, s, SB_WIDTH), BF16),
        compiler_params=pltpu.CompilerParams(
            dimension_semantics=("arbitrary", "arbitrary", "arbitrary"),
            vmem_limit_bytes=VMEM_LIMIT_BYTES),
        name="attn",
    )(q, k, v, tri, g_groups)


def _out_ffn_kernel(x_ref, conv_ref, attn_ref, wo_ref, gpm_ref, gpf_ref, wg_ref, wu_ref, wd_ref,
                    gpo_ref, o_ref):
    tm = x_ref.shape[0]
    half = tm // 2
    r0, r1 = slice(0, half), slice(half, tm)

    def head(r):
        y = (jnp.dot(conv_ref[r, :], wo_ref[0:CONV_CH, :], preferred_element_type=F32)
             + jnp.dot(attn_ref[r, :], wo_ref[CONV_CH:, :], preferred_element_type=F32))
        h = x_ref[r, :] + _rms(y, gpm_ref[...])
        return h, _rms(h, gpf_ref[...]).astype(BF16)

    def gate_up(f_in):
        gt = jnp.dot(f_in, wg_ref[...], preferred_element_type=F32)
        up = jnp.dot(f_in, wu_ref[...], preferred_element_type=F32)
        return (gt * jax.nn.sigmoid(gt) * up).astype(BF16)

    def down(r, h, act):
        f = jnp.dot(act, wd_ref[...], preferred_element_type=F32)
        o_ref[r, :] = h + _rms(f, gpo_ref[...])

    h0, f0 = head(r0)
    act0 = gate_up(f0)
    h1, f1 = head(r1)
    down(r0, h0, act0)
    act1 = gate_up(f1)
    down(r1, h1, act1)


def _out_ffn(x2, conv2, attn2, wo, gpm, gpf, wg, wu, wd, gpo, tm):
    n, d = x2.shape
    assert n % tm == 0, (n, tm)
    dff = wg.shape[1]
    const = lambda i: (0, 0)
    resident = dict(pipeline_mode=pl.Buffered(1))
    return pl.pallas_call(
        _out_ffn_kernel,
        grid=(n // tm,),
        in_specs=[
            pl.BlockSpec((tm, d), lambda i: (i, 0)),
            pl.BlockSpec((tm, CONV_CH), lambda i: (i, 0)),
            pl.BlockSpec((tm, SB_WIDTH), lambda i: (i, 0)),
            pl.BlockSpec((CONV_CH + SB_WIDTH, d), const, **resident),
            pl.BlockSpec((1, d), const),
            pl.BlockSpec((1, d), const),
            pl.BlockSpec((d, dff), const, **resident),
            pl.BlockSpec((d, dff), const, **resident),
            pl.BlockSpec((dff, d), const, **resident),
            pl.BlockSpec((1, d), const),
        ],
        out_specs=pl.BlockSpec((tm, d), lambda i: (i, 0)),
        out_shape=jax.ShapeDtypeStruct((n, d), F32),
        compiler_params=pltpu.CompilerParams(
            dimension_semantics=("arbitrary",), vmem_limit_bytes=VMEM_LIMIT_BYTES),
        name="out_ffn",
    )(x2, conv2, attn2, wo, gpm, gpf, wg, wu, wd, gpo)


def _layer(h, g_pre_mix, w_in, conv_w, conv_b, conv_ln_g, conv_ln_b, attn_norm_g, w_out,
           g_post_mix, g_pre_ffn, w_gate, w_up, w_down, g_post_ffn):
    b, s, d = h.shape
    n = b * s
    x2 = h.reshape(n, d)
    glu, q, k, v = _in_proj(x2, g_pre_mix.reshape(1, d), w_in.astype(BF16), min(IN_PROJ_ROWS, n))
    conv_out = _conv(glu.reshape(b, s, CONV_CH), conv_w.reshape(CONV_WIDTH, CONV_CH),
                     conv_b.reshape(1, CONV_CH), conv_ln_g.reshape(1, CONV_CH),
                     conv_ln_b.reshape(1, CONV_CH), ts=min(CONV_ROWS, s), rc=CONV_CHUNK)
    nh = ATTN_HEADS_PER_STEP
    attn_out = _attention(q.reshape(b, s, SB_WIDTH), k.reshape(b, s, SB_WIDTH),
                          v.reshape(b, s, SB_WIDTH),
                          attn_norm_g.reshape(SB_HEADS // nh, 1, nh * SB_HEAD_DIM),
                          t=min(ATTN_TILE, s // 2), nh=nh)
    out = _out_ffn(x2, conv_out.reshape(n, CONV_CH), attn_out.reshape(n, SB_WIDTH),
                   w_out.astype(BF16), g_post_mix.reshape(1, d), g_pre_ffn.reshape(1, d),
                   w_gate.astype(BF16), w_up.astype(BF16), w_down.astype(BF16),
                   g_post_ffn.reshape(1, d), min(OUT_FFN_ROWS, n))
    return out.reshape(b, s, d)


def kernel(x, g_pre_mix, w_in, conv_w, conv_b, conv_ln_g, conv_ln_b, attn_norm_g, w_out,
           g_post_mix, g_pre_ffn, w_gate, w_up, w_down, g_post_ffn):
    h = x
    for l in range(g_pre_mix.shape[0]):
        h = _layer(h, g_pre_mix[l], w_in[l], conv_w[l], conv_b[l], conv_ln_g[l], conv_ln_b[l],
                   attn_norm_g[l], w_out[l], g_post_mix[l], g_pre_ffn[l], w_gate[l], w_up[l],
                   w_down[l], g_post_ffn[l])
    return h
```

```python
import functools
import math

import jax
import jax.numpy as jnp
from jax import lax
from jax.experimental import pallas as pl
from jax.experimental.pallas import tpu as pltpu

EPS = 1e-6
CONV_CH = 512
CONV_WIDTH = 31
SB_HEADS = 8
SB_HEAD_DIM = 64
SB_WIDTH = SB_HEADS * SB_HEAD_DIM
ATTN_HEADS_PER_STEP = 4
ATTN_TILE = 256
ATTN_QUERY_TILES = 8
IN_PROJ_ROWS = 1024
OUT_FFN_ROWS = 1024
CONV_ROWS = 1024
CONV_CHUNK = 32
MASKED_LOGIT = -1e30
SWEEP_STOP_LOG2 = 160.0
CONV_HALO = 32
SUBLANES = 8

VMEM_LIMIT_BYTES = 56 * 1024 * 1024

LOG2_E = 1.4426950408889634

F32 = jnp.float32
BF16 = jnp.bfloat16


def _rms(x, g):
    return x * lax.rsqrt(jnp.mean(x * x, axis=-1, keepdims=True) + EPS) * g


def _in_proj_kernel(x_ref, g_ref, w_ref, glu_ref, q_ref, k_ref, v_ref, *, q_scale):
    a = _rms(x_ref[...], g_ref[...])
    u = jnp.dot(a.astype(BF16), w_ref[...], preferred_element_type=F32)
    val = u[:, :CONV_CH]
    gate = u[:, CONV_CH:2 * CONV_CH]
    glu_ref[...] = val * jax.nn.sigmoid(gate)
    o = 2 * CONV_CH
    q_ref[...] = (u[:, o:o + SB_WIDTH] * q_scale).astype(BF16)
    k_ref[...] = u[:, o + SB_WIDTH:o + 2 * SB_WIDTH].astype(BF16)
    v_ref[...] = u[:, o + 2 * SB_WIDTH:o + 3 * SB_WIDTH].astype(BF16)


def _in_proj(x2, g, w_bf16, tm):
    n, d = x2.shape
    assert n % tm == 0, (n, tm)
    cols = w_bf16.shape[1]
    kern = functools.partial(_in_proj_kernel, q_scale=LOG2_E / math.sqrt(SB_HEAD_DIM))
    return pl.pallas_call(
        kern,
        grid=(n // tm,),
        in_specs=[
            pl.BlockSpec((tm, d), lambda i: (i, 0)),
            pl.BlockSpec((1, d), lambda i: (0, 0)),
            pl.BlockSpec((d, cols), lambda i: (0, 0)),
        ],
        out_specs=[
            pl.BlockSpec((tm, CONV_CH), lambda i: (i, 0)),
            pl.BlockSpec((tm, SB_WIDTH), lambda i: (i, 0)),
            pl.BlockSpec((tm, SB_WIDTH), lambda i: (i, 0)),
            pl.BlockSpec((tm, SB_WIDTH), lambda i: (i, 0)),
        ],
        out_shape=[
            jax.ShapeDtypeStruct((n, CONV_CH), F32),
            jax.ShapeDtypeStruct((n, SB_WIDTH), BF16),
            jax.ShapeDtypeStruct((n, SB_WIDTH), BF16),
            jax.ShapeDtypeStruct((n, SB_WIDTH), BF16),
        ],
        compiler_params=pltpu.CompilerParams(
            dimension_semantics=("arbitrary",), vmem_limit_bytes=VMEM_LIMIT_BYTES),
        name="in_proj",
    )(x2, g, w_bf16)


def _conv_kernel(glu_ref, halo_ref, cw_ref, cb_ref, lg_ref, lb_ref, o_ref, buf_ref, sh_ref, *,
                 ts, rc):
    i = pl.program_id(1)
    buf_ref[0:CONV_HALO, :] = jnp.where(i > 0, halo_ref[0], 0.0)
    buf_ref[CONV_HALO:, :] = glu_ref[0]
    cb = cb_ref[...]
    lg = lg_ref[...]
    lb = lb_ref[...]
    base = CONV_HALO - (CONV_WIDTH - 1)
    for b in range(SUBLANES):
        span = ts + SUBLANES * ((CONV_WIDTH - 1 - b) // SUBLANES)
        sh_ref[b, 0:span // SUBLANES] = buf_ref[pl.ds(base + b, span), :].reshape(
            span // SUBLANES, SUBLANES, CONV_CH)
    tr, tc = ts // SUBLANES, rc // SUBLANES
    for r in range(0, tr, tc):
        acc = jnp.broadcast_to(cb, (tc, SUBLANES, CONV_CH))
        for w in range(CONV_WIDTH):
            a, b = divmod(w, SUBLANES)
            acc = acc + sh_ref[b, r + a:r + a + tc] * cw_ref[w]
        mu = jnp.mean(acc, axis=-1, keepdims=True)
        cen = acc - mu
        var = jnp.mean(cen * cen, axis=-1, keepdims=True)
        y = cen * lax.rsqrt(var + EPS) * lg + lb
        o_ref[0, r * SUBLANES:(r + tc) * SUBLANES, :] = (
            (y * jax.nn.sigmoid(y)).reshape(rc, CONV_CH).astype(o_ref.dtype))


def _conv(glu, cw, cb, lg, lb, ts, rc):
    b, s, c = glu.shape
    assert s % ts == 0 and ts % rc == 0 and ts % CONV_HALO == 0, (s, ts, rc)
    kern = functools.partial(_conv_kernel, ts=ts, rc=rc)
    hb = ts // CONV_HALO
    return pl.pallas_call(
        kern,
        grid=(b, s // ts),
        in_specs=[
            pl.BlockSpec((1, ts, c), lambda bi, i: (bi, i, 0)),
            pl.BlockSpec((1, CONV_HALO, c), lambda bi, i: (bi, jnp.maximum(i * hb - 1, 0), 0)),
            pl.BlockSpec((CONV_WIDTH, SUBLANES, c), lambda bi, i: (0, 0, 0)),
            pl.BlockSpec((1, c), lambda bi, i: (0, 0)),
            pl.BlockSpec((1, c), lambda bi, i: (0, 0)),
            pl.BlockSpec((1, c), lambda bi, i: (0, 0)),
        ],
        out_specs=pl.BlockSpec((1, ts, c), lambda bi, i: (bi, i, 0)),
        out_shape=jax.ShapeDtypeStruct((b, s, c), BF16),
        scratch_shapes=[
            pltpu.VMEM((ts + CONV_HALO, c), F32),
            pltpu.VMEM((SUBLANES, ts // SUBLANES + (CONV_WIDTH - 1) // SUBLANES, SUBLANES, c), F32),
        ],
        compiler_params=pltpu.CompilerParams(
            dimension_semantics=("arbitrary", "arbitrary"), vmem_limit_bytes=VMEM_LIMIT_BYTES),
        name="conv",
    )(glu, glu, jnp.broadcast_to(cw[:, None, :], (CONV_WIDTH, SUBLANES, c)), cb, lg, lb)


def _attn_kernel(q_ref, k_ref, v_ref, tri_ref, g_ref, o_ref, *, t, nh, nq):
    i = pl.program_id(2)
    qb = q_ref[0]
    lane = lax.broadcasted_iota(jnp.int32, (1, nh * SB_HEAD_DIM), 1)
    zero = jnp.zeros((), BF16)
    in_head = [(lane >= h * SB_HEAD_DIM) & (lane < (h + 1) * SB_HEAD_DIM) for h in range(nh)]
    q_heads = [jnp.where(m, qb, zero) for m in in_head]
    tri = tri_ref[...]
    first = nq * i

    def tile_rows(g, n=1):
        return slice(g * t, (g + n) * t)

    def key_tile(ref, j):
        j = jnp.maximum(j, 0)
        return ref[0, pl.ds(pl.multiple_of(j * t, t), t), :]

    def logits(rows, j):
        kt = key_tile(k_ref, j)
        return [lax.dot_general(q_heads[h][rows], kt, (((1,), (1,)), ((), ())),
                                preferred_element_type=F32) for h in range(nh)]

    def scores(zs, causal=None):
        out_z, css, tots = [], [], []
        for z in zs:
            sp = jnp.maximum(z, 0.0) + jnp.log(1.0 + jnp.exp2(-jnp.abs(z))) * LOG2_E
            if causal is not None:
                sp = jnp.where(causal, sp, 0.0)
                z = jnp.where(causal, z, MASKED_LOGIT)
            cs = jnp.dot(sp.astype(BF16), tri, preferred_element_type=F32)
            out_z.append(z)
            css.append(cs)
            tots.append(cs[:, 0:1])
        return out_z, css, tots

    def weighted_values(j, zs, css, carries, acc):
        vt = key_tile(v_ref, j)
        for h in range(nh):
            p = jnp.exp2(zs[h] - css[h] - carries[h])
            acc = acc + jnp.dot(p.astype(BF16), jnp.where(in_head[h], vt, zero),
                                preferred_element_type=F32)
        return acc

    def min_carry(carries):
        return jnp.min(functools.reduce(jnp.minimum, carries))

    def add(xs, ys):
        return tuple(x + y for x, y in zip(xs, ys))

    causal = (lax.broadcasted_iota(jnp.int32, (t, t), 1)
              < lax.broadcasted_iota(jnp.int32, (t, t), 0))
    c0 = (jnp.zeros((t, 1), F32),) * nh
    acc0 = jnp.zeros((t, nh * SB_HEAD_DIM), F32)
    no_second = jnp.where(i == 0, jnp.inf, 0.0).astype(F32)
    acc, carry, diag_tot = [None] * nq, [None] * nq, [None] * nq
    z_second = logits(tile_rows(0), first - 1)
    for g in range(nq):
        n_shared = 2 if g + 1 < nq else 1
        z_shared = logits(tile_rows(g, n_shared), first + g)
        z_d, cs_d, diag_tot[g] = scores([z[0:t] for z in z_shared], causal)
        acc[g] = weighted_values(first + g, z_d, cs_d, c0, acc0)
        z_s, cs_s, tot_s = scores(z_second)
        before = tuple(s + no_second for s in diag_tot[g]) if g == 0 else diag_tot[g]
        acc[g] = weighted_values(first + g - 1, z_s, cs_s, before, acc[g])
        carry[g] = add(diag_tot[g], tot_s)
        if n_shared == 2:
            z_second = [z[t:2 * t] for z in z_shared]

    def write_normalized(accs):
        for g in range(nq):
            a = accs[g]
            sq = a * a
            inv = jnp.zeros_like(a)
            for h in range(nh):
                ms = jnp.sum(jnp.where(in_head[h], sq, 0.0), axis=-1, keepdims=True) * (1.0 / SB_HEAD_DIM)
                inv = jnp.where(in_head[h], lax.rsqrt(ms + EPS), inv)
            o_ref[0, tile_rows(g), :] = (a * inv * g_ref[0]).astype(o_ref.dtype)

    def lowest(m, carries):
        low = jnp.float32(jnp.inf)
        for g in range(nq):
            left = first + g - 2 - m >= 0
            low = jnp.minimum(low, jnp.where(left, min_carry(carries[g]), jnp.inf))
        return low

    def more(state):
        m, _, _, low = state
        return (first + nq - 3 - m >= 0) & (low < SWEEP_STOP_LOG2)

    def step(state):
        m, accs, carries, _ = state
        accs, carries = list(accs), list(carries)
        for g in range(nq):
            j = first + g - 2 - m
            z, cs, tot = scores(logits(tile_rows(g), j))
            done = jnp.where(j < 0, jnp.inf, 0.0).astype(F32)
            accs[g] = weighted_values(j, z, cs, tuple(c + done for c in carries[g]), accs[g])
            carries[g] = add(carries[g], tot)
        return m + 1, tuple(accs), tuple(carries), lowest(m + 1, carries)

    write_normalized(acc)
    low = lowest(0, carry)

    @pl.when(low < SWEEP_STOP_LOG2)
    def _():
        state = (jnp.int32(0), tuple(acc), tuple(carry), low)
        _, acc_end, _, _ = lax.while_loop(more, step, state)
        write_normalized(acc_end)


def _attention(q, k, v, g_groups, t, nh, nq):
    b, s, _ = q.shape
    assert s % (nq * t) == 0 and SB_HEADS % nh == 0, (s, t, nh, nq)
    w = nh * SB_HEAD_DIM
    tri = (lax.broadcasted_iota(jnp.int32, (t, t), 0)
           >= lax.broadcasted_iota(jnp.int32, (t, t), 1)).astype(BF16)
    kern = functools.partial(_attn_kernel, t=t, nh=nh, nq=nq)
    return pl.pallas_call(
        kern,
        grid=(b, SB_HEADS // nh, s // (nq * t)),
        in_specs=[
            pl.BlockSpec((1, nq * t, w), lambda bi, p, i: (bi, i, p)),
            pl.BlockSpec((1, s, w), lambda bi, p, i: (bi, 0, p)),
            pl.BlockSpec((1, s, w), lambda bi, p, i: (bi, 0, p)),
            pl.BlockSpec((t, t), lambda bi, p, i: (0, 0)),
            pl.BlockSpec((1, 1, w), lambda bi, p, i: (p, 0, 0)),
        ],
        out_specs=pl.BlockSpec((1, nq * t, w), lambda bi, p, i: (bi, i, p)),
        out_shape=jax.ShapeDtypeStruct((b, s, SB_WIDTH), BF16),
        compiler_params=pltpu.CompilerParams(
            dimension_semantics=("arbitrary", "arbitrary", "arbitrary"),
            vmem_limit_bytes=VMEM_LIMIT_BYTES),
        name="attn",
    )(q, k, v, tri, g_groups)


def _out_ffn_kernel(x_ref, conv_ref, attn_ref, wo_ref, gpm_ref, gpf_ref, wg_ref, wu_ref, wd_ref,
                    gpo_ref, o_ref):
    tm = x_ref.shape[0]
    half = tm // 2
    r0, r1 = slice(0, half), slice(half, tm)

    def head(r):
        y = (jnp.dot(conv_ref[r, :], wo_ref[0:CONV_CH, :], preferred_element_type=F32)
             + jnp.dot(attn_ref[r, :], wo_ref[CONV_CH:, :], preferred_element_type=F32))
        h = x_ref[r, :] + _rms(y, gpm_ref[...])
        return h, _rms(h, gpf_ref[...]).astype(BF16)

    def gate_up(f_in):
        gt = jnp.dot(f_in, wg_ref[...], preferred_element_type=F32)
        up = jnp.dot(f_in, wu_ref[...], preferred_element_type=F32)
        return (gt * jax.nn.sigmoid(gt) * up).astype(BF16)

    def down(r, h, act):
        f = jnp.dot(act, wd_ref[...], preferred_element_type=F32)
        o_ref[r, :] = h + _rms(f, gpo_ref[...])

    h0, f0 = head(r0)
    act0 = gate_up(f0)
    h1, f1 = head(r1)
    down(r0, h0, act0)
    act1 = gate_up(f1)
    down(r1, h1, act1)


def _out_ffn(x2, conv2, attn2, wo, gpm, gpf, wg, wu, wd, gpo, tm):
    n, d = x2.shape
    assert n % tm == 0, (n, tm)
    dff = wg.shape[1]
    const = lambda i: (0, 0)
    resident = dict(pipeline_mode=pl.Buffered(1))
    return pl.pallas_call(
        _out_ffn_kernel,
        grid=(n // tm,),
        in_specs=[
            pl.BlockSpec((tm, d), lambda i: (i, 0)),
            pl.BlockSpec((tm, CONV_CH), lambda i: (i, 0)),
            pl.BlockSpec((tm, SB_WIDTH), lambda i: (i, 0)),
            pl.BlockSpec((CONV_CH + SB_WIDTH, d), const, **resident),
            pl.BlockSpec((1, d), const),
            pl.BlockSpec((1, d), const),
            pl.BlockSpec((d, dff), const, **resident),
            pl.BlockSpec((d, dff), const, **resident),
            pl.BlockSpec((dff, d), const, **resident),
            pl.BlockSpec((1, d), const),
        ],
        out_specs=pl.BlockSpec((tm, d), lambda i: (i, 0)),
        out_shape=jax.ShapeDtypeStruct((n, d), F32),
        compiler_params=pltpu.CompilerParams(
            dimension_semantics=("arbitrary",), vmem_limit_bytes=VMEM_LIMIT_BYTES),
        name="out_ffn",
    )(x2, conv2, attn2, wo, gpm, gpf, wg, wu, wd, gpo)


def _layer(h, g_pre_mix, w_in, conv_w, conv_b, conv_ln_g, conv_ln_b, attn_norm_g, w_out,
           g_post_mix, g_pre_ffn, w_gate, w_up, w_down, g_post_ffn):
    b, s, d = h.shape
    n = b * s
    x2 = h.reshape(n, d)
    glu, q, k, v = _in_proj(x2, g_pre_mix.reshape(1, d), w_in.astype(BF16), min(IN_PROJ_ROWS, n))
    conv_out = _conv(glu.reshape(b, s, CONV_CH), conv_w.reshape(CONV_WIDTH, CONV_CH),
                     conv_b.reshape(1, CONV_CH), conv_ln_g.reshape(1, CONV_CH),
                     conv_ln_b.reshape(1, CONV_CH), ts=min(CONV_ROWS, s), rc=CONV_CHUNK)
    nh = ATTN_HEADS_PER_STEP
    attn_out = _attention(q.reshape(b, s, SB_WIDTH), k.reshape(b, s, SB_WIDTH),
                          v.reshape(b, s, SB_WIDTH),
                          attn_norm_g.reshape(SB_HEADS // nh, 1, nh * SB_HEAD_DIM),
                          t=min(ATTN_TILE, s // ATTN_QUERY_TILES), nh=nh, nq=ATTN_QUERY_TILES)
    out = _out_ffn(x2, conv_out.reshape(n, CONV_CH), attn_out.reshape(n, SB_WIDTH),
                   w_out.astype(BF16), g_post_mix.reshape(1, d), g_pre_ffn.reshape(1, d),
                   w_gate.astype(BF16), w_up.astype(BF16), w_down.astype(BF16),
                   g_post_ffn.reshape(1, d), min(OUT_FFN_ROWS, n))
    return out.reshape(b, s, d)


def kernel(x, g_pre_mix, w_in, conv_w, conv_b, conv_ln_g, conv_ln_b, attn_norm_g, w_out,
           g_post_mix, g_pre_ffn, w_gate, w_up, w_down, g_post_ffn):
    h = x
    for l in range(g_pre_mix.shape[0]):
        h = _layer(h, g_pre_mix[l], w_in[l], conv_w[l], conv_b[l], conv_ln_g[l], conv_ln_b[l],
                   attn_norm_g[l], w_out[l], g_post_mix[l], g_pre_ffn[l], w_gate[l], w_up[l],
                   w_down[l], g_post_ffn[l])
    return h
```

```python
import functools
import math

import jax
import jax.numpy as jnp
from jax import lax
from jax.experimental import pallas as pl
from jax.experimental.pallas import tpu as pltpu

EPS = 1e-6
CONV_CH = 512
CONV_WIDTH = 31
SB_HEADS = 8
SB_HEAD_DIM = 64
SB_WIDTH = SB_HEADS * SB_HEAD_DIM
ATTN_HEADS_PER_STEP = 4
ATTN_TILE = 256
ATTN_QUERY_TILES = 8
IN_PROJ_ROWS = 1024
OUT_FFN_ROWS = 1024
CONV_ROWS = 128
CONV_CHUNK = 32
MASKED_LOGIT = -1e30
SWEEP_STOP_LOG2 = 160.0
CONV_HALO = 32
SUBLANES = 8

VMEM_LIMIT_BYTES = 56 * 1024 * 1024

LOG2_E = 1.4426950408889634

F32 = jnp.float32
BF16 = jnp.bfloat16


def _rms(x, g):
    return x * lax.rsqrt(jnp.mean(x * x, axis=-1, keepdims=True) + EPS) * g


def _in_proj_conv_kernel(x_ref, g_ref, w_ref, cw_ref, cb_ref, lg_ref, lb_ref,
                         conv_ref, q_ref, k_ref, v_ref, buf_ref, sh_ref, *,
                         q_scale, tiles_per_seq, sub, rc):
    i = pl.program_id(0)
    tm = x_ref.shape[0]
    n_sub = tm // sub

    @pl.when(i % tiles_per_seq == 0)
    def _():
        buf_ref[0:CONV_HALO, :] = jnp.zeros((CONV_HALO, CONV_CH), F32)

    def project(k):
        r = slice(k * sub, (k + 1) * sub)
        a = _rms(x_ref[r, :], g_ref[...])
        u = jnp.dot(a.astype(BF16), w_ref[...], preferred_element_type=F32)
        val = u[:, :CONV_CH]
        gate = u[:, CONV_CH:2 * CONV_CH]
        buf_ref[CONV_HALO + k * sub:CONV_HALO + (k + 1) * sub, :] = val * jax.nn.sigmoid(gate)
        o = 2 * CONV_CH
        q_ref[r, :] = (u[:, o:o + SB_WIDTH] * q_scale).astype(BF16)
        k_ref[r, :] = u[:, o + SB_WIDTH:o + 2 * SB_WIDTH].astype(BF16)
        v_ref[r, :] = u[:, o + 2 * SB_WIDTH:o + 3 * SB_WIDTH].astype(BF16)

    cb = cb_ref[...]
    lg = lg_ref[...]
    lb = lb_ref[...]
    base = CONV_HALO - (CONV_WIDTH - 1)
    tr, tc = sub // SUBLANES, rc // SUBLANES

    def conv(k):
        s0 = k * sub
        for b in range(SUBLANES):
            span = sub + SUBLANES * ((CONV_WIDTH - 1 - b) // SUBLANES)
            sh_ref[b, 0:span // SUBLANES] = buf_ref[pl.ds(s0 + base + b, span), :].reshape(
                span // SUBLANES, SUBLANES, CONV_CH)
        for r in range(0, tr, tc):
            acc = jnp.broadcast_to(cb, (tc, SUBLANES, CONV_CH))
            for w in range(CONV_WIDTH):
                a, b = divmod(w, SUBLANES)
                acc = acc + sh_ref[b, r + a:r + a + tc] * cw_ref[w]
            mu = jnp.mean(acc, axis=-1, keepdims=True)
            cen = acc - mu
            var = jnp.mean(cen * cen, axis=-1, keepdims=True)
            y = cen * lax.rsqrt(var + EPS) * lg + lb
            conv_ref[s0 + r * SUBLANES:s0 + (r + tc) * SUBLANES, :] = (
                (y * jax.nn.sigmoid(y)).reshape(rc, CONV_CH).astype(conv_ref.dtype))

    project(0)
    for k in range(n_sub):
        if k + 1 < n_sub:
            project(k + 1)
        conv(k)
    buf_ref[0:CONV_HALO, :] = buf_ref[tm:tm + CONV_HALO, :]


def _in_proj_conv(x2, g, w_bf16, cw, cb, lg, lb, tm, tiles_per_seq, sub, rc):
    n, d = x2.shape
    assert n % tm == 0 and tm % sub == 0 and sub % rc == 0, (n, tm, sub, rc)
    cols = w_bf16.shape[1]
    kern = functools.partial(_in_proj_conv_kernel, q_scale=LOG2_E / math.sqrt(SB_HEAD_DIM),
                             tiles_per_seq=tiles_per_seq, sub=sub, rc=rc)
    const = lambda i: (0, 0)
    return pl.pallas_call(
        kern,
        grid=(n // tm,),
        in_specs=[
            pl.BlockSpec((tm, d), lambda i: (i, 0)),
            pl.BlockSpec((1, d), const),
            pl.BlockSpec((d, cols), const),
            pl.BlockSpec((CONV_WIDTH, SUBLANES, CONV_CH), lambda i: (0, 0, 0)),
            pl.BlockSpec((1, CONV_CH), const),
            pl.BlockSpec((1, CONV_CH), const),
            pl.BlockSpec((1, CONV_CH), const),
        ],
        out_specs=[
            pl.BlockSpec((tm, CONV_CH), lambda i: (i, 0)),
            pl.BlockSpec((tm, SB_WIDTH), lambda i: (i, 0)),
            pl.BlockSpec((tm, SB_WIDTH), lambda i: (i, 0)),
            pl.BlockSpec((tm, SB_WIDTH), lambda i: (i, 0)),
        ],
        out_shape=[
            jax.ShapeDtypeStruct((n, CONV_CH), BF16),
            jax.ShapeDtypeStruct((n, SB_WIDTH), BF16),
            jax.ShapeDtypeStruct((n, SB_WIDTH), BF16),
            jax.ShapeDtypeStruct((n, SB_WIDTH), BF16),
        ],
        scratch_shapes=[
            pltpu.VMEM((tm + CONV_HALO, CONV_CH), F32),
            pltpu.VMEM((SUBLANES, sub // SUBLANES + (CONV_WIDTH - 1) // SUBLANES, SUBLANES, CONV_CH),
                       F32),
        ],
        compiler_params=pltpu.CompilerParams(
            dimension_semantics=("arbitrary",), vmem_limit_bytes=VMEM_LIMIT_BYTES),
        name="in_proj_conv",
    )(x2, g, w_bf16, jnp.broadcast_to(cw[:, None, :], (CONV_WIDTH, SUBLANES, CONV_CH)), cb, lg, lb)


def _attn_kernel(q_ref, k_ref, v_ref, tri_ref, g_ref, o_ref, *, t, nh, nq):
    i = pl.program_id(2)
    qb = q_ref[0]
    lane = lax.broadcasted_iota(jnp.int32, (1, nh * SB_HEAD_DIM), 1)
    zero = jnp.zeros((), BF16)
    in_head = [(lane >= h * SB_HEAD_DIM) & (lane < (h + 1) * SB_HEAD_DIM) for h in range(nh)]
    q_heads = [jnp.where(m, qb, zero) for m in in_head]
    tri = tri_ref[...]
    first = nq * i

    def tile_rows(g, n=1):
        return slice(g * t, (g + n) * t)

    def key_tile(ref, j):
        j = jnp.maximum(j, 0)
        return ref[0, pl.ds(pl.multiple_of(j * t, t), t), :]

    def logits(rows, j):
        kt = key_tile(k_ref, j)
        return [lax.dot_general(q_heads[h][rows], kt, (((1,), (1,)), ((), ())),
                                preferred_element_type=F32) for h in range(nh)]

    def scores(zs, causal=None):
        out_z, css, tots = [], [], []
        for z in zs:
            sp = jnp.maximum(z, 0.0) + jnp.log(1.0 + jnp.exp2(-jnp.abs(z))) * LOG2_E
            if causal is not None:
                sp = jnp.where(causal, sp, 0.0)
                z = jnp.where(causal, z, MASKED_LOGIT)
            cs = jnp.dot(sp.astype(BF16), tri, preferred_element_type=F32)
            out_z.append(z)
            css.append(cs)
            tots.append(cs[:, 0:1])
        return out_z, css, tots

    def weighted_values(j, zs, css, carries, acc):
        vt = key_tile(v_ref, j)
        for h in range(nh):
            p = jnp.exp2(zs[h] - css[h] - carries[h])
            acc = acc + jnp.dot(p.astype(BF16), jnp.where(in_head[h], vt, zero),
                                preferred_element_type=F32)
        return acc

    def min_carry(carries):
        return jnp.min(functools.reduce(jnp.minimum, carries))

    def add(xs, ys):
        return tuple(x + y for x, y in zip(xs, ys))

    causal = (lax.broadcasted_iota(jnp.int32, (t, t), 1)
              < lax.broadcasted_iota(jnp.int32, (t, t), 0))
    c0 = (jnp.zeros((t, 1), F32),) * nh
    acc0 = jnp.zeros((t, nh * SB_HEAD_DIM), F32)
    no_second = jnp.where(i == 0, jnp.inf, 0.0).astype(F32)
    acc, carry, diag_tot = [None] * nq, [None] * nq, [None] * nq
    z_second = logits(tile_rows(0), first - 1)
    for g in range(nq):
        n_shared = 2 if g + 1 < nq else 1
        z_shared = logits(tile_rows(g, n_shared), first + g)
        z_d, cs_d, diag_tot[g] = scores([z[0:t] for z in z_shared], causal)
        acc[g] = weighted_values(first + g, z_d, cs_d, c0, acc0)
        z_s, cs_s, tot_s = scores(z_second)
        before = tuple(s + no_second for s in diag_tot[g]) if g == 0 else diag_tot[g]
        acc[g] = weighted_values(first + g - 1, z_s, cs_s, before, acc[g])
        carry[g] = add(diag_tot[g], tot_s)
        if n_shared == 2:
            z_second = [z[t:2 * t] for z in z_shared]

    def write_normalized(accs):
        for g in range(nq):
            a = accs[g]
            sq = a * a
            inv = jnp.zeros_like(a)
            for h in range(nh):
                ms = jnp.sum(jnp.where(in_head[h], sq, 0.0), axis=-1, keepdims=True) * (1.0 / SB_HEAD_DIM)
                inv = jnp.where(in_head[h], lax.rsqrt(ms + EPS), inv)
            o_ref[0, tile_rows(g), :] = (a * inv * g_ref[0]).astype(o_ref.dtype)

    def lowest(m, carries):
        low = jnp.float32(jnp.inf)
        for g in range(nq):
            left = first + g - 2 - m >= 0
            low = jnp.minimum(low, jnp.where(left, min_carry(carries[g]), jnp.inf))
        return low

    def more(state):
        m, _, _, low = state
        return (first + nq - 3 - m >= 0) & (low < SWEEP_STOP_LOG2)

    def step(state):
        m, accs, carries, _ = state
        accs, carries = list(accs), list(carries)
        for g in range(nq):
            j = first + g - 2 - m
            z, cs, tot = scores(logits(tile_rows(g), j))
            done = jnp.where(j < 0, jnp.inf, 0.0).astype(F32)
            accs[g] = weighted_values(j, z, cs, tuple(c + done for c in carries[g]), accs[g])
            carries[g] = add(carries[g], tot)
        return m + 1, tuple(accs), tuple(carries), lowest(m + 1, carries)

    write_normalized(acc)
    low = lowest(0, carry)

    @pl.when(low < SWEEP_STOP_LOG2)
    def _():
        state = (jnp.int32(0), tuple(acc), tuple(carry), low)
        _, acc_end, _, _ = lax.while_loop(more, step, state)
        write_normalized(acc_end)


def _attention(q, k, v, g_groups, t, nh, nq):
    b, s, _ = q.shape
    assert s % (nq * t) == 0 and SB_HEADS % nh == 0, (s, t, nh, nq)
    w = nh * SB_HEAD_DIM
    tri = (lax.broadcasted_iota(jnp.int32, (t, t), 0)
           >= lax.broadcasted_iota(jnp.int32, (t, t), 1)).astype(BF16)
    kern = functools.partial(_attn_kernel, t=t, nh=nh, nq=nq)
    return pl.pallas_call(
        kern,
        grid=(b, SB_HEADS // nh, s // (nq * t)),
        in_specs=[
            pl.BlockSpec((1, nq * t, w), lambda bi, p, i: (bi, i, p)),
            pl.BlockSpec((1, s, w), lambda bi, p, i: (bi, 0, p)),
            pl.BlockSpec((1, s, w), lambda bi, p, i: (bi, 0, p)),
            pl.BlockSpec((t, t), lambda bi, p, i: (0, 0)),
            pl.BlockSpec((1, 1, w), lambda bi, p, i: (p, 0, 0)),
        ],
        out_specs=pl.BlockSpec((1, nq * t, w), lambda bi, p, i: (bi, i, p)),
        out_shape=jax.ShapeDtypeStruct((b, s, SB_WIDTH), BF16),
        compiler_params=pltpu.CompilerParams(
            dimension_semantics=("arbitrary", "arbitrary", "arbitrary"),
            vmem_limit_bytes=VMEM_LIMIT_BYTES),
        name="attn",
    )(q, k, v, tri, g_groups)


def _out_ffn_kernel(x_ref, conv_ref, attn_ref, wo_ref, gpm_ref, gpf_ref, wg_ref, wu_ref, wd_ref,
                    gpo_ref, o_ref):
    tm = x_ref.shape[0]
    half = tm // 2
    r0, r1 = slice(0, half), slice(half, tm)

    def head(r):
        y = (jnp.dot(conv_ref[r, :], wo_ref[0:CONV_CH, :], preferred_element_type=F32)
             + jnp.dot(attn_ref[r, :], wo_ref[CONV_CH:, :], preferred_element_type=F32))
        h = x_ref[r, :] + _rms(y, gpm_ref[...])
        return h, _rms(h, gpf_ref[...]).astype(BF16)

    def gate_up(f_in):
        gt = jnp.dot(f_in, wg_ref[...], preferred_element_type=F32)
        up = jnp.dot(f_in, wu_ref[...], preferred_element_type=F32)
        return (gt * jax.nn.sigmoid(gt) * up).astype(BF16)

    def down(r, h, act):
        f = jnp.dot(act, wd_ref[...], preferred_element_type=F32)
        o_ref[r, :] = h + _rms(f, gpo_ref[...])

    h0, f0 = head(r0)
    act0 = gate_up(f0)
    h1, f1 = head(r1)
    down(r0, h0, act0)
    act1 = gate_up(f1)
    down(r1, h1, act1)


def _out_ffn(x2, conv2, attn2, wo, gpm, gpf, wg, wu, wd, gpo, tm):
    n, d = x2.shape
    assert n % tm == 0, (n, tm)
    dff = wg.shape[1]
    const = lambda i: (0, 0)
    resident = dict(pipeline_mode=pl.Buffered(1))
    return pl.pallas_call(
        _out_ffn_kernel,
        grid=(n // tm,),
        in_specs=[
            pl.BlockSpec((tm, d), lambda i: (i, 0)),
            pl.BlockSpec((tm, CONV_CH), lambda i: (i, 0)),
            pl.BlockSpec((tm, SB_WIDTH), lambda i: (i, 0)),
            pl.BlockSpec((CONV_CH + SB_WIDTH, d), const, **resident),
            pl.BlockSpec((1, d), const),
            pl.BlockSpec((1, d), const),
            pl.BlockSpec((d, dff), const, **resident),
            pl.BlockSpec((d, dff), const, **resident),
            pl.BlockSpec((dff, d), const, **resident),
            pl.BlockSpec((1, d), const),
        ],
        out_specs=pl.BlockSpec((tm, d), lambda i: (i, 0)),
        out_shape=jax.ShapeDtypeStruct((n, d), F32),
        compiler_params=pltpu.CompilerParams(
            dimension_semantics=("arbitrary",), vmem_limit_bytes=VMEM_LIMIT_BYTES),
        name="out_ffn",
    )(x2, conv2, attn2, wo, gpm, gpf, wg, wu, wd, gpo)


def _layer(h, g_pre_mix, w_in, conv_w, conv_b, conv_ln_g, conv_ln_b, attn_norm_g, w_out,
           g_post_mix, g_pre_ffn, w_gate, w_up, w_down, g_post_ffn):
    b, s, d = h.shape
    n = b * s
    x2 = h.reshape(n, d)
    tm = min(IN_PROJ_ROWS, s)
    conv_out, q, k, v = _in_proj_conv(
        x2, g_pre_mix.reshape(1, d), w_in.astype(BF16), conv_w.reshape(CONV_WIDTH, CONV_CH),
        conv_b.reshape(1, CONV_CH), conv_ln_g.reshape(1, CONV_CH), conv_ln_b.reshape(1, CONV_CH),
        tm, tiles_per_seq=s // tm, sub=min(CONV_ROWS, tm), rc=CONV_CHUNK)
    nh = ATTN_HEADS_PER_STEP
    attn_out = _attention(q.reshape(b, s, SB_WIDTH), k.reshape(b, s, SB_WIDTH),
                          v.reshape(b, s, SB_WIDTH),
                          attn_norm_g.reshape(SB_HEADS // nh, 1, nh * SB_HEAD_DIM),
                          t=min(ATTN_TILE, s // ATTN_QUERY_TILES), nh=nh, nq=ATTN_QUERY_TILES)
    out = _out_ffn(x2, conv_out.reshape(n, CONV_CH), attn_out.reshape(n, SB_WIDTH),
                   w_out.astype(BF16), g_post_mix.reshape(1, d), g_pre_ffn.reshape(1, d),
                   w_gate.astype(BF16), w_up.astype(BF16), w_down.astype(BF16),
                   g_post_ffn.reshape(1, d), min(OUT_FFN_ROWS, n))
    return out.reshape(b, s, d)


def kernel(x, g_pre_mix, w_in, conv_w, conv_b, conv_ln_g, conv_ln_b, attn_norm_g, w_out,
           g_post_mix, g_pre_ffn, w_gate, w_up, w_down, g_post_ffn):
    h = x
    for l in range(g_pre_mix.shape[0]):
        h = _layer(h, g_pre_mix[l], w_in[l], conv_w[l], conv_b[l], conv_ln_g[l], conv_ln_b[l],
                   attn_norm_g[l], w_out[l], g_post_mix[l], g_pre_ffn[l], w_gate[l], w_up[l],
                   w_down[l], g_post_ffn[l])
    return h
```
